```python
import math
import jax, jax.numpy as jnp
from jax import lax
import numpy as np

D_MODEL = 1024
BATCH = 4
SEQ = 4096
DEPTH = 4

CHUNK = 64
N_META = 16
Q_BLOCK = 128
N_MIXERS = 2
MLA_HEADS = 16
QK_NOPE = 64
QK_ROPE = 32
V_HEAD = 64
Q_LORA = 384
KV_LORA = 256
MLA_WIDTH = MLA_HEADS * V_HEAD
MLA_IN = Q_LORA + KV_LORA + QK_ROPE + MLA_WIDTH
ROPE_BASE = 10000.0
CONV_WIDTH = D_MODEL
CONV_K = 3
CONV_IN = 4 * CONV_WIDTH
DN_ALPHA = (2 * DEPTH) ** 0.25
DN_BETA = (8 * DEPTH) ** -0.25
N_MLA_LAYERS = (DEPTH + 1) // 2
N_CONV_LAYERS = DEPTH // 2
LN_EPS = 1e-5
RMS_EPS = 1e-6
NEG_INF = -1e30

kernel_name = 'chunked_mla_shortconv_deepnorm_trunk'


def layer_norm(x, g, b):
    xf = x.astype(jnp.float32)
    mu = jnp.mean(xf, axis=-1, keepdims=True)
    var = jnp.mean(jnp.square(xf - mu), axis=-1, keepdims=True)
    y = (xf - mu) * lax.rsqrt(var + LN_EPS) * g.astype(jnp.float32) + b.astype(jnp.float32)
    return y.astype(x.dtype)


def rms_norm(x, g):
    xf = x.astype(jnp.float32)
    y = xf * lax.rsqrt(jnp.mean(jnp.square(xf), axis=-1, keepdims=True) + RMS_EPS)
    return (y * g.astype(jnp.float32)).astype(x.dtype)


def rope(x, cos, sin):
    x1, x2 = jnp.split(x, 2, axis=-1)
    return jnp.concatenate([x1 * cos - x2 * sin, x1 * sin + x2 * cos], axis=-1)


def mla_mixer(h, w_in, q_norm_g, w_uq, kv_norm_g, w_uk, w_uv, w_o, cos, sin):
    bsz, L, _ = h.shape
    proj = h @ w_in
    c_q, c_kv, k_rope, z = jnp.split(
        proj, [Q_LORA, Q_LORA + KV_LORA, Q_LORA + KV_LORA + QK_ROPE], axis=-1)
    q = (rms_norm(c_q, q_norm_g) @ w_uq).reshape(bsz, L, MLA_HEADS, QK_NOPE + QK_ROPE)
    q_nope = q[..., :QK_NOPE]
    q_rope = rope(q[..., QK_NOPE:], cos[:, :, None, :], sin[:, :, None, :])
    c_kv = rms_norm(c_kv, kv_norm_g)
    k_nope = (c_kv @ w_uk).reshape(bsz, L, MLA_HEADS, QK_NOPE)
    v = (c_kv @ w_uv).reshape(bsz, L, MLA_HEADS, V_HEAD)
    k_rope = rope(k_rope, cos, sin)
    frame_chunk = 1 + jnp.arange(SEQ, dtype=jnp.int32) // CHUNK
    key_chunk = jnp.concatenate([jnp.zeros((N_META,), jnp.int32), frame_chunk])
    scale = (QK_NOPE + QK_ROPE) ** -0.5

    def attend(qn, qr, q_chunk):
        s = (jnp.einsum('bqhd,bkhd->bhqk', qn, k_nope)
             + jnp.einsum('bqhr,bkr->bhqk', qr, k_rope)).astype(jnp.float32) * scale
        mask = key_chunk[None, :] <= q_chunk[:, None]
        s = jnp.where(mask[None, None], s, NEG_INF)
        p = jax.nn.softmax(s, axis=-1).astype(v.dtype)
        return jnp.einsum('bhqk,bkhd->bqhd', p, v)

    o_meta = attend(q_nope[:, :N_META], q_rope[:, :N_META],
                    jnp.zeros((N_META,), jnp.int32))
    n_blk = SEQ // Q_BLOCK
    qn_f = q_nope[:, N_META:].reshape(bsz, n_blk, Q_BLOCK, MLA_HEADS, QK_NOPE).transpose(1, 0, 2, 3, 4)
    qr_f = q_rope[:, N_META:].reshape(bsz, n_blk, Q_BLOCK, MLA_HEADS, QK_ROPE).transpose(1, 0, 2, 3, 4)
    qc_f = frame_chunk.reshape(n_blk, Q_BLOCK)
    o_f = lax.map(lambda a: attend(a[0], a[1], a[2]), (qn_f, qr_f, qc_f))
    o_f = o_f.transpose(1, 0, 2, 3, 4).reshape(bsz, SEQ, MLA_HEADS, V_HEAD)
    o = jnp.concatenate([o_meta, o_f], axis=1).reshape(bsz, L, MLA_WIDTH)
    return (o * jax.nn.silu(z)) @ w_o


def conv_mixer(h, w_in, conv_w, w_out):
    L = h.shape[1]
    b_gate, c_gate, u, z = jnp.split(h @ w_in, 4, axis=-1)
    cu = jnp.pad(c_gate * u, ((0, 0), (CONV_K - 1, 0), (0, 0)))
    conv = cu[:, 0:L] * conv_w[0]
    for k in range(1, CONV_K):
        conv = conv + cu[:, k:k + L] * conv_w[k]
    return (b_gate * conv * jax.nn.silu(z)) @ w_out


def setup_inputs(seed: int = 0) -> dict:
    key = jax.random.key(seed)
    ks = jax.random.split(key, 20)
    f32 = jnp.float32
    nrm = lambda k, shape, s: jax.random.normal(k, shape, f32) * s
    x = nrm(ks[0], (BATCH, SEQ, D_MODEL), 1.0)
    offset = jax.random.randint(ks[1], (BATCH, 1), 0, 8, dtype=jnp.int32) * CHUNK
    positions = (offset + jnp.arange(SEQ, dtype=jnp.int32)[None, :]).astype(jnp.int32)
    meta_tokens = nrm(ks[2], (N_META, D_MODEL), 1.0)
    ln_g = 1.0 + nrm(ks[3], (DEPTH, D_MODEL), 0.02)
    ln_b = nrm(ks[4], (DEPTH, D_MODEL), 0.02)
    mla_w_in = nrm(ks[5], (N_MLA_LAYERS, D_MODEL, MLA_IN), D_MODEL ** -0.5)
    mla_q_norm_g = 1.0 + nrm(ks[6], (N_MLA_LAYERS, Q_LORA), 0.02)
    mla_w_uq = nrm(ks[7], (N_MLA_LAYERS, Q_LORA, MLA_HEADS * (QK_NOPE + QK_ROPE)), Q_LORA ** -0.5)
    mla_kv_norm_g = 1.0 + nrm(ks[8], (N_MLA_LAYERS, KV_LORA), 0.02)
    mla_w_uk = nrm(ks[9], (N_MLA_LAYERS, KV_LORA, MLA_HEADS * QK_NOPE), KV_LORA ** -0.5)
    mla_w_uv = nrm(ks[10], (N_MLA_LAYERS, KV_LORA, MLA_HEADS * V_HEAD), KV_LORA ** -0.5 * DN_BETA)
    mla_w_o = nrm(ks[11], (N_MLA_LAYERS, MLA_WIDTH, D_MODEL), MLA_WIDTH ** -0.5 * DN_BETA)
    conv_w_in = nrm(ks[12], (N_CONV_LAYERS, D_MODEL, CONV_IN), D_MODEL ** -0.5)
    conv_w = nrm(ks[13], (N_CONV_LAYERS, CONV_K, CONV_WIDTH), CONV_K ** -0.5)
    conv_w_out = nrm(ks[14], (N_CONV_LAYERS, CONV_WIDTH, D_MODEL), CONV_WIDTH ** -0.5 * DN_BETA)
    return {'x': x, 'positions': positions, 'meta_tokens': meta_tokens,
            'ln_g': ln_g, 'ln_b': ln_b,
            'mla_w_in': mla_w_in, 'mla_q_norm_g': mla_q_norm_g, 'mla_w_uq': mla_w_uq,
            'mla_kv_norm_g': mla_kv_norm_g, 'mla_w_uk': mla_w_uk, 'mla_w_uv': mla_w_uv,
            'mla_w_o': mla_w_o,
            'conv_w_in': conv_w_in, 'conv_w': conv_w, 'conv_w_out': conv_w_out}


def reference(x, positions, meta_tokens, ln_g, ln_b,
              mla_w_in, mla_q_norm_g, mla_w_uq, mla_kv_norm_g, mla_w_uk, mla_w_uv, mla_w_o,
              conv_w_in, conv_w, conv_w_out):
    bsz = x.shape[0]
    meta = jnp.broadcast_to(meta_tokens[None].astype(x.dtype), (bsz, N_META, D_MODEL))
    h = jnp.concatenate([meta, x], axis=1)
    meta_pos = jnp.broadcast_to(jnp.arange(N_META, dtype=jnp.int32)[None], (bsz, N_META))
    rope_pos = jnp.concatenate([meta_pos, positions + N_META], axis=1).astype(jnp.float32)
    inv_freq = ROPE_BASE ** (-jnp.arange(0, QK_ROPE, 2, dtype=jnp.float32) / QK_ROPE)
    ang = rope_pos[..., None] * inv_freq
    cos = jnp.cos(ang).astype(x.dtype)
    sin = jnp.sin(ang).astype(x.dtype)
    for i in range(DEPTH):
        j = i // N_MIXERS
        if i % N_MIXERS == 0:
            out = mla_mixer(h, mla_w_in[j], mla_q_norm_g[j], mla_w_uq[j], mla_kv_norm_g[j],
                            mla_w_uk[j], mla_w_uv[j], mla_w_o[j], cos, sin)
        else:
            out = conv_mixer(h, conv_w_in[j], conv_w[j], conv_w_out[j])
        h = layer_norm(DN_ALPHA * h + out, ln_g[i], ln_b[i])
    return h[:, N_META:]
```

```python
import functools

import jax
import jax.numpy as jnp
from jax import lax
from jax.experimental import pallas as pl
from jax.experimental.pallas import tpu as pltpu

D_MODEL = 1024
DEPTH = 4
CHUNK = 64
CHUNK_SHIFT = CHUNK.bit_length() - 1
N_META = 16
MLA_HEADS = 16
QK_NOPE = 64
QK_ROPE = 32
HALF_ROPE = QK_ROPE // 2
V_HEAD = 64
Q_LORA = 384
KV_LORA = 256
MLA_WIDTH = MLA_HEADS * V_HEAD
ROPE_BASE = 10000.0
CONV_WIDTH = D_MODEL
CONV_K = 3
DN_ALPHA = (2 * DEPTH) ** 0.25
LN_EPS = 1e-5
RMS_EPS = 1e-6
NEG_INF = -1e30

HEAD_SLOT = 128
QK_WIDTH = MLA_HEADS * HEAD_SLOT
META_PAD = 128
TOKEN_TILE = 256
SUBLANES = 8
VMEM_LIMIT = 56 * 1024 * 1024

_OFF_CKV = Q_LORA
_OFF_Z = Q_LORA + KV_LORA
_OFF_RA = _OFF_Z + MLA_WIDTH
_OFF_RB = _OFF_RA + HEAD_SLOT
_W1_COLS = _OFF_RB + HEAD_SLOT

_NT = (((1,), (1,)), ((), ()))


def _bf16(x):
    return x.astype(jnp.bfloat16)


def _dot(a, b):
    return jnp.dot(a, b, preferred_element_type=jnp.float32)


def _dot_nt(a, b):
    return lax.dot_general(a, b, _NT, preferred_element_type=jnp.float32)


def _rms(x, g):
    y = x * lax.rsqrt(jnp.mean(jnp.square(x), axis=-1, keepdims=True) + RMS_EPS)
    return y * g


def _layer_norm(x, g, b):
    mu = jnp.mean(x, axis=-1, keepdims=True)
    xc = x - mu
    var = jnp.mean(jnp.square(xc), axis=-1, keepdims=True)
    return xc * lax.rsqrt(var + LN_EPS) * g + b


def _silu(z):
    return z * (1.0 / (1.0 + jnp.exp(-z)))


def _mla_proj_kernel(h_ref, cos_t_ref, sin_t_ref, cos_s_ref, sin_s_ref,
                     w1_ref, gq_ref, gkv_ref, wuq_ref, wuk_ref, wuv_ref,
                     qt_ref, k_ref, vt_ref, gate_ref):
    hb = _bf16(h_ref[0])
    p = _dot(hb, w1_ref[...])
    cqn = _bf16(_rms(p[:, :Q_LORA], gq_ref[...]))
    ckvn = _bf16(_rms(p[:, _OFF_CKV:_OFF_Z], gkv_ref[...]))
    gate_ref[0] = _bf16(_silu(p[:, _OFF_Z:_OFF_RA]))

    kr_slot = p[:, _OFF_RA:_OFF_RB] * cos_s_ref[0] + p[:, _OFF_RB:_W1_COLS] * sin_s_ref[0]
    k_all = _dot(ckvn, wuk_ref[...])
    for hd in range(MLA_HEADS):
        k_ref[0, hd] = _bf16(k_all[:, hd * HEAD_SLOT:(hd + 1) * HEAD_SLOT] + kr_slot)

    vt_ref[0, 0] = _bf16(_dot_nt(wuv_ref[...], ckvn))

    scale = (QK_NOPE + QK_ROPE) ** -0.5
    qt = _dot_nt(wuq_ref[...], cqn) * scale
    cos_t = cos_t_ref[0]
    sin_t = sin_t_ref[0]
    for hd in range(MLA_HEADS):
        base = hd * HEAD_SLOT
        x1 = qt[base + QK_NOPE:base + QK_NOPE + HALF_ROPE]
        x2 = qt[base + QK_NOPE + HALF_ROPE:base + QK_NOPE + QK_ROPE]
        qt_ref[0, base:base + QK_NOPE, :] = _bf16(qt[base:base + QK_NOPE])
        qt_ref[0, base + QK_NOPE:base + QK_NOPE + HALF_ROPE, :] = _bf16(x1 * cos_t - x2 * sin_t)
        qt_ref[0, base + QK_NOPE + HALF_ROPE:base + QK_NOPE + QK_ROPE, :] = _bf16(x1 * sin_t + x2 * cos_t)
        qt_ref[0, base + QK_NOPE + QK_ROPE:base + HEAD_SLOT, :] = jnp.zeros(
            (HEAD_SLOT - QK_NOPE - QK_ROPE, qt.shape[1]), jnp.bfloat16)


def _mla_proj(h, cos_t, sin_t, cos_s, sin_s, w, tm):
    bx, lx, _ = h.shape
    nt = lx // tm
    tile = lambda b, i: (b, i, 0)
    tile_t = lambda b, i: (b, 0, i)
    const2 = lambda b, i: (0, 0)
    return pl.pallas_call(
        _mla_proj_kernel,
        grid=(bx, nt),
        in_specs=[
            pl.BlockSpec((1, tm, D_MODEL), tile),
            pl.BlockSpec((1, HALF_ROPE, tm), tile_t),
            pl.BlockSpec((1, HALF_ROPE, tm), tile_t),
            pl.BlockSpec((1, tm, HEAD_SLOT), tile),
            pl.BlockSpec((1, tm, HEAD_SLOT), tile),
            pl.BlockSpec((D_MODEL, _W1_COLS), const2),
            pl.BlockSpec((1, Q_LORA), const2),
            pl.BlockSpec((1, KV_LORA), const2),
            pl.BlockSpec((QK_WIDTH, Q_LORA), const2),
            pl.BlockSpec((KV_LORA, QK_WIDTH), const2),
            pl.BlockSpec((MLA_WIDTH, KV_LORA), const2),
        ],
        out_specs=[
            pl.BlockSpec((1, QK_WIDTH, tm), tile_t),
            pl.BlockSpec((1, MLA_HEADS, tm, HEAD_SLOT), lambda b, i: (b, 0, i, 0)),
            pl.BlockSpec((1, 1, MLA_WIDTH, tm), lambda b, i: (b, i, 0, 0)),
            pl.BlockSpec((1, tm, MLA_WIDTH), tile),
        ],
        out_shape=[
            jax.ShapeDtypeStruct((bx, QK_WIDTH, lx), jnp.bfloat16),
            jax.ShapeDtypeStruct((bx, MLA_HEADS, lx, HEAD_SLOT), jnp.bfloat16),
            jax.ShapeDtypeStruct((bx, nt, MLA_WIDTH, tm), jnp.bfloat16),
            jax.ShapeDtypeStruct((bx, lx, MLA_WIDTH), jnp.bfloat16),
        ],
        compiler_params=pltpu.CompilerParams(
            dimension_semantics=("parallel", "parallel"), vmem_limit_bytes=VMEM_LIMIT),
        name="mla_proj",
    )(h, cos_t, sin_t, cos_s, sin_s, w["w1"], w["gq"], w["gkv"], w["wuq_t"], w["wuk"], w["wuv_t"])


def _softmax_start(s):
    m = jnp.max(s, axis=0, keepdims=True)
    p = jnp.exp(s - m)
    return m, jnp.sum(p, axis=0, keepdims=True), p


def _meta_scores(q, km_ref):
    s = _dot(km_ref[0, 0], q)
    key = lax.broadcasted_iota(jnp.int32, s.shape, 0)
    return jnp.where(key < N_META, s, NEG_INF)


def _meta_attn_kernel(qt_ref, km_ref, vm_ref, ot_ref):
    q = qt_ref[0]
    _, l, p = _softmax_start(_meta_scores(q, km_ref))
    ot_ref[0] = _bf16(_dot(vm_ref[0, 0], _bf16(p)) / l)


def _frame_attn_kernel(qt_ref, k_ref, vt_ref, km_ref, vm_ref, ot_ref):
    tk = vt_ref.shape[3]
    qi = pl.program_id(2)
    q = qt_ref[0]

    m, l, p = _softmax_start(_meta_scores(q, km_ref))
    acc = _dot(vm_ref[0, 0], _bf16(p))

    def step(j, carry, masked):
        m, l, acc = carry
        kt = k_ref[0, 0, pl.ds(pl.multiple_of(j * tk, tk), tk), :]
        s = _dot(kt, q)
        if masked:
            key = lax.broadcasted_iota(jnp.int32, s.shape, 0)
            qry = lax.broadcasted_iota(jnp.int32, s.shape, 1)
            s = jnp.where(key >> CHUNK_SHIFT <= qry >> CHUNK_SHIFT, s, NEG_INF)
        m_new = jnp.maximum(m, jnp.max(s, axis=0, keepdims=True))
        alpha = jnp.exp(m - m_new)
        p = jnp.exp(s - m_new)
        l = alpha * l + jnp.sum(p, axis=0, keepdims=True)
        acc = alpha * acc + _dot(vt_ref[0, j], _bf16(p))
        return m_new, l, acc

    carry = lax.fori_loop(0, qi, functools.partial(step, masked=False), (m, l, acc))
    m, l, acc = step(qi, carry, masked=True)
    ot_ref[0] = _bf16(acc / l)


def _meta_attn(qt, k, vt):
    return pl.pallas_call(
        _meta_attn_kernel,
        grid=(MLA_HEADS,),
        in_specs=[
            pl.BlockSpec((1, HEAD_SLOT, META_PAD), lambda h: (0, h, 0)),
            pl.BlockSpec((1, 1, META_PAD, HEAD_SLOT), lambda h: (0, h, 0, 0)),
            pl.BlockSpec((1, 1, V_HEAD, META_PAD), lambda h: (0, 0, h, 0)),
        ],
        out_specs=pl.BlockSpec((1, V_HEAD, META_PAD), lambda h: (0, h, 0)),
        out_shape=jax.ShapeDtypeStruct((1, MLA_WIDTH, META_PAD), jnp.bfloat16),
        name="meta_attn",
    )(qt, k, vt)


def _frame_attn(qt, k, vt, k_meta, vt_meta, tq):
    bx, _, lx = qt.shape
    nk, tk = vt.shape[1], vt.shape[3]
    assert tq == tk
    return pl.pallas_call(
        _frame_attn_kernel,
        grid=(bx, MLA_HEADS, lx // tq),
        in_specs=[
            pl.BlockSpec((1, HEAD_SLOT, tq), lambda b, h, i: (b, h, i)),
            pl.BlockSpec((1, 1, lx, HEAD_SLOT), lambda b, h, i: (b, h, 0, 0)),
            pl.BlockSpec((1, nk, V_HEAD, tk), lambda b, h, i: (b, 0, h, 0)),
            pl.BlockSpec((1, 1, META_PAD, HEAD_SLOT), lambda b, h, i: (0, h, 0, 0)),
            pl.BlockSpec((1, 1, V_HEAD, META_PAD), lambda b, h, i: (0, 0, h, 0)),
        ],
        out_specs=pl.BlockSpec((1, V_HEAD, tq), lambda b, h, i: (b, h, i)),
        out_shape=jax.ShapeDtypeStruct((bx, MLA_WIDTH, lx), jnp.bfloat16),
        compiler_params=pltpu.CompilerParams(
            dimension_semantics=("parallel", "parallel", "arbitrary"), vmem_limit_bytes=VMEM_LIMIT),
        name="frame_attn",
    )(qt, k, vt, k_meta, vt_meta)


def _mla_out_kernel(ot_ref, gate_ref, h_ref, wo_ref, g_ref, b_ref, out_ref):
    o = ot_ref[0].astype(jnp.float32).T
    y = _bf16(o * gate_ref[0].astype(jnp.float32))
    r = DN_ALPHA * h_ref[0] + _dot(y, wo_ref[...])
    out_ref[0] = _layer_norm(r, g_ref[...], b_ref[...])


def _mla_out(ot, gate, h, wo, g, b, tm):
    bx, lx, _ = h.shape
    tile = lambda b_, i: (b_, i, 0)
    const2 = lambda b_, i: (0, 0)
    return pl.pallas_call(
        _mla_out_kernel,
        grid=(bx, lx // tm),
        in_specs=[
            pl.BlockSpec((1, MLA_WIDTH, tm), lambda b_, i: (b_, 0, i)),
            pl.BlockSpec((1, tm, MLA_WIDTH), tile),
            pl.BlockSpec((1, tm, D_MODEL), tile),
            pl.BlockSpec((MLA_WIDTH, D_MODEL), const2),
            pl.BlockSpec((1, D_MODEL), const2),
            pl.BlockSpec((1, D_MODEL), const2),
        ],
        out_specs=pl.BlockSpec((1, tm, D_MODEL), tile),
        out_shape=jax.ShapeDtypeStruct(h.shape, jnp.float32),
        compiler_params=pltpu.CompilerParams(
            dimension_semantics=("parallel", "parallel"), vmem_limit_bytes=VMEM_LIMIT),
        name="mla_out",
    )(ot, gate, h, wo, g, b)


def _conv_kernel(h_ref, carry_ref, win_ref, cw_ref, wout_ref, g_ref, b_ref,
                 out_ref, tail_ref, cu_ref, *, tail_start):
    tm = h_ref.shape[1]
    w = CONV_WIDTH

    @pl.when(pl.program_id(1) == 0)
    def _():
        cu_ref[0:SUBLANES, :] = carry_ref[0]

    h = h_ref[0]
    hb = _bf16(h)
    cu = _dot(hb, win_ref[:, w:2 * w]) * _dot(hb, win_ref[:, 2 * w:3 * w])
    cu_ref[SUBLANES:SUBLANES + tm, :] = cu
    conv = cu_ref[SUBLANES - 2:SUBLANES - 2 + tm, :] * cw_ref[0:1, :]
    conv = conv + cu_ref[SUBLANES - 1:SUBLANES - 1 + tm, :] * cw_ref[1:2, :]
    conv = conv + cu * cw_ref[2:3, :]
    tail_ref[0] = cu_ref[SUBLANES + tail_start:2 * SUBLANES + tail_start, :]
    cu_ref[0:SUBLANES, :] = cu_ref[tm:tm + SUBLANES, :]

    y = _dot(hb, win_ref[:, 0:w]) * conv * _silu(_dot(hb, win_ref[:, 3 * w:4 * w]))
    r = DN_ALPHA * h + _dot(_bf16(y), wout_ref[...])
    out_ref[0] = _layer_norm(r, g_ref[...], b_ref[...])


def _conv_layer(h, carry, win, cw, wout, g, b, tm, tail_start):
    bx, lx, _ = h.shape
    tile = lambda b_, i: (b_, i, 0)
    const2 = lambda b_, i: (0, 0)
    return pl.pallas_call(
        functools.partial(_conv_kernel, tail_start=tail_start),
        grid=(bx, lx // tm),
        in_specs=[
            pl.BlockSpec((1, tm, D_MODEL), tile),
            pl.BlockSpec((1, SUBLANES, CONV_WIDTH), lambda b_, i: (0, 0, 0)),
            pl.BlockSpec((D_MODEL, 4 * CONV_WIDTH), const2),
            pl.BlockSpec((SUBLANES, CONV_WIDTH), const2),
            pl.BlockSpec((CONV_WIDTH, D_MODEL), const2),
            pl.BlockSpec((1, D_MODEL), const2),
            pl.BlockSpec((1, D_MODEL), const2),
        ],
        out_specs=[
            pl.BlockSpec((1, tm, D_MODEL), tile),
            pl.BlockSpec((1, SUBLANES, CONV_WIDTH), lambda b_, i: (b_, 0, 0)),
        ],
        out_shape=[
            jax.ShapeDtypeStruct(h.shape, jnp.float32),
            jax.ShapeDtypeStruct((bx, SUBLANES, CONV_WIDTH), jnp.float32),
        ],
        scratch_shapes=[pltpu.VMEM((tm + SUBLANES, CONV_WIDTH), jnp.float32)],
        compiler_params=pltpu.CompilerParams(
            dimension_semantics=("arbitrary", "arbitrary"), vmem_limit_bytes=VMEM_LIMIT),
        name="conv_layer",
    )(h, carry, win, cw, wout, g, b)


def _prep_mla_weights(w_in, gq, w_uq, gkv, w_uk, w_uv, w_o):
    o_kr = Q_LORA + KV_LORA
    o_z = o_kr + QK_ROPE
    kr1 = w_in[:, o_kr:o_kr + HALF_ROPE]
    kr2 = w_in[:, o_kr + HALF_ROPE:o_z]
    zl = jnp.zeros((D_MODEL, QK_NOPE), w_in.dtype)
    zr = jnp.zeros((D_MODEL, HEAD_SLOT - QK_NOPE - QK_ROPE), w_in.dtype)
    w1 = jnp.concatenate([w_in[:, :o_kr], w_in[:, o_z:],
                          zl, kr1, kr2, zr, zl, -kr2, kr1, zr], axis=1)
    wuq = w_uq.reshape(Q_LORA, MLA_HEADS, QK_NOPE + QK_ROPE)
    wuq = jnp.pad(wuq, ((0, 0), (0, 0), (0, HEAD_SLOT - QK_NOPE - QK_ROPE)))
    wuk = jnp.pad(w_uk.reshape(KV_LORA, MLA_HEADS, QK_NOPE), ((0, 0), (0, 0), (0, HEAD_SLOT - QK_NOPE)))
    return {
        "w1": _bf16(w1),
        "gq": gq.reshape(1, Q_LORA),
        "gkv": gkv.reshape(1, KV_LORA),
        "wuq_t": _bf16(wuq.reshape(Q_LORA, QK_WIDTH).T),
        "wuk": _bf16(wuk.reshape(KV_LORA, QK_WIDTH)),
        "wuv_t": _bf16(w_uv.T),
        "wo": _bf16(w_o),
    }


def _rope_tables(pos):
    inv_freq = ROPE_BASE ** (-jnp.arange(0, QK_ROPE, 2, dtype=jnp.float32) / QK_ROPE)
    ang = pos[..., None] * inv_freq
    cos, sin = jnp.cos(ang), jnp.sin(ang)

    def slot(t):
        lead = ((0, 0),) * (t.ndim - 1)
        return jnp.pad(jnp.concatenate([t, t], axis=-1),
                       lead + ((QK_NOPE, HEAD_SLOT - QK_NOPE - QK_ROPE),))

    return jnp.swapaxes(cos, 1, 2), jnp.swapaxes(sin, 1, 2), slot(cos), slot(sin)


def kernel(x, positions, meta_tokens, ln_g, ln_b, mla_w_in, mla_q_norm_g, mla_w_uq, mla_kv_norm_g,
           mla_w_uk, mla_w_uv, mla_w_o, conv_w_in, conv_w, conv_w_out):
    bsz, seq, _ = x.shape
    assert seq % TOKEN_TILE == 0 and TOKEN_TILE % CHUNK == 0
    f32 = jnp.float32

    hf = x.astype(f32)
    hm = jnp.pad(meta_tokens.astype(f32), ((0, META_PAD - N_META), (0, 0)))[None]

    frame_tabs = _rope_tables((positions + N_META).astype(f32))
    meta_pos = jnp.pad(jnp.arange(N_META, dtype=f32), (0, META_PAD - N_META))[None]
    meta_tabs = _rope_tables(meta_pos)

    for i in range(DEPTH):
        j = i // 2
        g = ln_g[i].reshape(1, D_MODEL)
        b = ln_b[i].reshape(1, D_MODEL)
        if i % 2 == 0:
            w = _prep_mla_weights(mla_w_in[j], mla_q_norm_g[j], mla_w_uq[j], mla_kv_norm_g[j],
                                  mla_w_uk[j], mla_w_uv[j], mla_w_o[j])
            qt_m, k_m, vt_m, gate_m = _mla_proj(hm, *meta_tabs, w, META_PAD)
            qt_f, k_f, vt_f, gate_f = _mla_proj(hf, *frame_tabs, w, TOKEN_TILE)
            ot_m = _meta_attn(qt_m, k_m, vt_m)
            ot_f = _frame_attn(qt_f, k_f, vt_f, k_m, vt_m, TOKEN_TILE)
            hm = _mla_out(ot_m, gate_m, hm, w["wo"], g, b, META_PAD)
            hf = _mla_out(ot_f, gate_f, hf, w["wo"], g, b, TOKEN_TILE)
        else:
            win = _bf16(conv_w_in[j])
            wout = _bf16(conv_w_out[j])
            cw = jnp.pad(conv_w[j].astype(f32), ((0, SUBLANES - CONV_K), (0, 0)))
            zero_carry = jnp.zeros((1, SUBLANES, CONV_WIDTH), f32)
            hm, tail = _conv_layer(hm, zero_carry, win, cw, wout, g, b, META_PAD, N_META - SUBLANES)
            hf, _ = _conv_layer(hf, tail, win, cw, wout, g, b, TOKEN_TILE, 0)
    return hf
```

```python
import functools

import jax
import jax.numpy as jnp
from jax import lax
from jax.experimental import pallas as pl
from jax.experimental.pallas import tpu as pltpu

D_MODEL = 1024
DEPTH = 4
CHUNK = 64
CHUNK_SHIFT = CHUNK.bit_length() - 1
N_META = 16
MLA_HEADS = 16
QK_NOPE = 64
QK_ROPE = 32
HALF_ROPE = QK_ROPE // 2
V_HEAD = 64
Q_LORA = 384
KV_LORA = 256
MLA_WIDTH = MLA_HEADS * V_HEAD
ROPE_BASE = 10000.0
CONV_WIDTH = D_MODEL
CONV_K = 3
DN_ALPHA = (2 * DEPTH) ** 0.25
LN_EPS = 1e-5
RMS_EPS = 1e-6
NEG_INF = -1e30
LOG2_E = 1.4426950408889634

HEAD_SLOT = 128
QK_WIDTH = MLA_HEADS * HEAD_SLOT
META_PAD = 128
TOKEN_TILE = 256
SUBLANES = 8
VMEM_LIMIT = 56 * 1024 * 1024

_OFF_CKV = Q_LORA
_OFF_Z = Q_LORA + KV_LORA
_OFF_RA = _OFF_Z + MLA_WIDTH
_OFF_RB = _OFF_RA + HEAD_SLOT
_W1_COLS = _OFF_RB + HEAD_SLOT

_NT = (((1,), (1,)), ((), ()))


def _bf16(x):
    return x.astype(jnp.bfloat16)


def _dot(a, b):
    return jnp.dot(a, b, preferred_element_type=jnp.float32)


def _dot_nt(a, b):
    return lax.dot_general(a, b, _NT, preferred_element_type=jnp.float32)


def _rms(x, g):
    y = x * lax.rsqrt(jnp.mean(jnp.square(x), axis=-1, keepdims=True) + RMS_EPS)
    return y * g


def _layer_norm(x, g, b):
    mu = jnp.mean(x, axis=-1, keepdims=True)
    xc = x - mu
    var = jnp.mean(jnp.square(xc), axis=-1, keepdims=True)
    return xc * lax.rsqrt(var + LN_EPS) * g + b


def _silu(z):
    return z * (1.0 / (1.0 + jnp.exp(-z)))


def _mla_proj_kernel(h_ref, cos_t_ref, sin_t_ref, cos_s_ref, sin_s_ref,
                     w1_ref, gq_ref, gkv_ref, wuq_ref, wuk_ref, wuv_ref,
                     qt_ref, k_ref, vt_ref, gate_ref):
    hb = _bf16(h_ref[0])
    p = _dot(hb, w1_ref[...])
    cqn = _bf16(_rms(p[:, :Q_LORA], gq_ref[...]))
    ckvn = _bf16(_rms(p[:, _OFF_CKV:_OFF_Z], gkv_ref[...]))
    gate_ref[0] = _bf16(_silu(p[:, _OFF_Z:_OFF_RA]))

    kr_slot = p[:, _OFF_RA:_OFF_RB] * cos_s_ref[0] + p[:, _OFF_RB:_W1_COLS] * sin_s_ref[0]
    k_all = _dot(ckvn, wuk_ref[...])
    for hd in range(MLA_HEADS):
        k_ref[0, hd] = _bf16(k_all[:, hd * HEAD_SLOT:(hd + 1) * HEAD_SLOT] + kr_slot)

    vt_ref[0, 0] = _bf16(_dot_nt(wuv_ref[...], ckvn))

    scale = (QK_NOPE + QK_ROPE) ** -0.5 * LOG2_E
    qt = _dot_nt(wuq_ref[...], cqn) * scale
    cos_t = cos_t_ref[0]
    sin_t = sin_t_ref[0]
    for hd in range(MLA_HEADS):
        base = hd * HEAD_SLOT
        x1 = qt[base + QK_NOPE:base + QK_NOPE + HALF_ROPE]
        x2 = qt[base + QK_NOPE + HALF_ROPE:base + QK_NOPE + QK_ROPE]
        qt_ref[0, base:base + QK_NOPE, :] = _bf16(qt[base:base + QK_NOPE])
        qt_ref[0, base + QK_NOPE:base + QK_NOPE + HALF_ROPE, :] = _bf16(x1 * cos_t - x2 * sin_t)
        qt_ref[0, base + QK_NOPE + HALF_ROPE:base + QK_NOPE + QK_ROPE, :] = _bf16(x1 * sin_t + x2 * cos_t)
        qt_ref[0, base + QK_NOPE + QK_ROPE:base + HEAD_SLOT, :] = jnp.zeros(
            (HEAD_SLOT - QK_NOPE - QK_ROPE, qt.shape[1]), jnp.bfloat16)


def _mla_proj(h, cos_t, sin_t, cos_s, sin_s, w, tm):
    bx, lx, _ = h.shape
    nt = lx // tm
    tile = lambda b, i: (b, i, 0)
    tile_t = lambda b, i: (b, 0, i)
    const2 = lambda b, i: (0, 0)
    return pl.pallas_call(
        _mla_proj_kernel,
        grid=(bx, nt),
        in_specs=[
            pl.BlockSpec((1, tm, D_MODEL), tile),
            pl.BlockSpec((1, HALF_ROPE, tm), tile_t),
            pl.BlockSpec((1, HALF_ROPE, tm), tile_t),
            pl.BlockSpec((1, tm, HEAD_SLOT), tile),
            pl.BlockSpec((1, tm, HEAD_SLOT), tile),
            pl.BlockSpec((D_MODEL, _W1_COLS), const2),
            pl.BlockSpec((1, Q_LORA), const2),
            pl.BlockSpec((1, KV_LORA), const2),
            pl.BlockSpec((QK_WIDTH, Q_LORA), const2),
            pl.BlockSpec((KV_LORA, QK_WIDTH), const2),
            pl.BlockSpec((MLA_WIDTH, KV_LORA), const2),
        ],
        out_specs=[
            pl.BlockSpec((1, QK_WIDTH, tm), tile_t),
            pl.BlockSpec((1, MLA_HEADS, tm, HEAD_SLOT), lambda b, i: (b, 0, i, 0)),
            pl.BlockSpec((1, 1, MLA_WIDTH, tm), lambda b, i: (b, i, 0, 0)),
            pl.BlockSpec((1, tm, MLA_WIDTH), tile),
        ],
        out_shape=[
            jax.ShapeDtypeStruct((bx, QK_WIDTH, lx), jnp.bfloat16),
            jax.ShapeDtypeStruct((bx, MLA_HEADS, lx, HEAD_SLOT), jnp.bfloat16),
            jax.ShapeDtypeStruct((bx, nt, MLA_WIDTH, tm), jnp.bfloat16),
            jax.ShapeDtypeStruct((bx, lx, MLA_WIDTH), jnp.bfloat16),
        ],
        compiler_params=pltpu.CompilerParams(
            dimension_semantics=("parallel", "parallel"), vmem_limit_bytes=VMEM_LIMIT),
        name="mla_proj",
    )(h, cos_t, sin_t, cos_s, sin_s, w["w1"], w["gq"], w["gkv"], w["wuq_t"], w["wuk"], w["wuv_t"])


def _softmax_start(s):
    m = jnp.max(s, axis=0, keepdims=True)
    p = jnp.exp2(s - m)
    return m, jnp.sum(p, axis=0, keepdims=True), p


def _meta_scores(q, km):
    s = _dot(km, q)
    key = lax.broadcasted_iota(jnp.int32, s.shape, 0)
    return jnp.where(key < N_META, s, NEG_INF)


def _head_rows(hd, width):
    return slice(hd * width, (hd + 1) * width)


def _meta_attn_kernel(qt_ref, km_ref, vm_ref, ot_ref):
    for hd in range(MLA_HEADS):
        q = qt_ref[0, _head_rows(hd, HEAD_SLOT), :]
        _, l, p = _softmax_start(_meta_scores(q, km_ref[0, hd]))
        ot_ref[0, _head_rows(hd, V_HEAD), :] = _bf16(
            _dot(vm_ref[0, 0, _head_rows(hd, V_HEAD), :], _bf16(p)) / l)


def _frame_attn_kernel(qt_ref, k_ref, vt_ref, km_ref, vm_ref, ot_ref,
                       sm_ref, s_ref, m_ref, l_ref, acc_ref):
    tk = vt_ref.shape[3]
    qi = pl.program_id(1)

    def q_head(hd):
        return qt_ref[0, _head_rows(hd, HEAD_SLOT), :]

    def frame_scores(j, hd, slot):
        row0 = pl.multiple_of(j * tk, tk)
        s_ref[slot, hd] = _dot(k_ref[0, hd, pl.ds(row0, tk), :], q_head(hd))

    def overlapped_step(j, slot):
        for hd in range(MLA_HEADS):
            frame_scores(j + 1, hd, 1 - slot)
            update(hd, s_ref[slot, hd], vt_ref[0, j, _head_rows(hd, V_HEAD), :])

    def update(hd, s, v):
        m = m_ref[hd]
        m_new = jnp.maximum(m, jnp.max(s, axis=0, keepdims=True))
        alpha = jnp.exp2(m - m_new)
        p = jnp.exp2(s - m_new)
        m_ref[hd] = m_new
        l_ref[hd] = alpha * l_ref[hd] + jnp.sum(p, axis=0, keepdims=True)
        acc_ref[hd] = alpha * acc_ref[hd] + _dot(v, _bf16(p))

    for hd in range(MLA_HEADS):
        sm_ref[hd] = _meta_scores(q_head(hd), km_ref[0, hd])

    for hd in range(MLA_HEADS):
        frame_scores(0, hd, 0)
        m, l, p = _softmax_start(sm_ref[hd])
        m_ref[hd] = m
        l_ref[hd] = l
        acc_ref[hd] = _dot(vm_ref[0, 0, _head_rows(hd, V_HEAD), :], _bf16(p))

    def pair_step(jj, carry):
        overlapped_step(2 * jj, 0)
        overlapped_step(2 * jj + 1, 1)
        return carry

    lax.fori_loop(0, qi // 2, pair_step, 0)

    @pl.when(qi % 2 == 1)
    def _():
        overlapped_step(qi - 1, 0)

    key = lax.broadcasted_iota(jnp.int32, s_ref.shape[2:], 0)
    qry = lax.broadcasted_iota(jnp.int32, s_ref.shape[2:], 1)
    visible = key >> CHUNK_SHIFT <= qry >> CHUNK_SHIFT
    for hd in range(MLA_HEADS):
        s = jnp.where(visible, s_ref[qi % 2, hd], NEG_INF)
        update(hd, s, vt_ref[0, qi, _head_rows(hd, V_HEAD), :])
    for hd in range(MLA_HEADS):
        ot_ref[0, _head_rows(hd, V_HEAD), :] = _bf16(acc_ref[hd] / l_ref[hd])


def _meta_attn(qt, k, vt):
    return pl.pallas_call(
        _meta_attn_kernel,
        grid=(1,),
        in_specs=[
            pl.BlockSpec((1, QK_WIDTH, META_PAD), lambda i: (0, 0, 0)),
            pl.BlockSpec((1, MLA_HEADS, META_PAD, HEAD_SLOT), lambda i: (0, 0, 0, 0)),
            pl.BlockSpec((1, 1, MLA_WIDTH, META_PAD), lambda i: (0, 0, 0, 0)),
        ],
        out_specs=pl.BlockSpec((1, MLA_WIDTH, META_PAD), lambda i: (0, 0, 0)),
        out_shape=jax.ShapeDtypeStruct((1, MLA_WIDTH, META_PAD), jnp.bfloat16),
        name="meta_attn",
    )(qt, k, vt)


def _frame_attn(qt, k, vt, k_meta, vt_meta, tq):
    bx, _, lx = qt.shape
    nk, tk = vt.shape[1], vt.shape[3]
    assert tq == tk
    once = pl.Buffered(1)
    return pl.pallas_call(
        _frame_attn_kernel,
        grid=(bx, lx // tq),
        in_specs=[
            pl.BlockSpec((1, QK_WIDTH, tq), lambda b, i: (b, 0, i)),
            pl.BlockSpec((1, MLA_HEADS, lx, HEAD_SLOT), lambda b, i: (b, 0, 0, 0), pipeline_mode=once),
            pl.BlockSpec((1, nk, MLA_WIDTH, tk), lambda b, i: (b, 0, 0, 0), pipeline_mode=once),
            pl.BlockSpec((1, MLA_HEADS, META_PAD, HEAD_SLOT), lambda b, i: (0, 0, 0, 0)),
            pl.BlockSpec((1, 1, MLA_WIDTH, META_PAD), lambda b, i: (0, 0, 0, 0)),
        ],
        out_specs=pl.BlockSpec((1, MLA_WIDTH, tq), lambda b, i: (b, 0, i)),
        out_shape=jax.ShapeDtypeStruct((bx, MLA_WIDTH, lx), jnp.bfloat16),
        scratch_shapes=[
            pltpu.VMEM((MLA_HEADS, META_PAD, tq), jnp.float32),
            pltpu.VMEM((2, MLA_HEADS, tk, tq), jnp.float32),
            pltpu.VMEM((MLA_HEADS, 1, tq), jnp.float32),
            pltpu.VMEM((MLA_HEADS, 1, tq), jnp.float32),
            pltpu.VMEM((MLA_HEADS, V_HEAD, tq), jnp.float32),
        ],
        compiler_params=pltpu.CompilerParams(
            dimension_semantics=("parallel", "arbitrary"), vmem_limit_bytes=VMEM_LIMIT),
        name="frame_attn",
    )(qt, k, vt, k_meta, vt_meta)


def _mla_out_kernel(ot_ref, gate_ref, h_ref, wo_ref, g_ref, b_ref, out_ref):
    o = ot_ref[0].astype(jnp.float32).T
    y = _bf16(o * gate_ref[0].astype(jnp.float32))
    r = DN_ALPHA * h_ref[0] + _dot(y, wo_ref[...])
    out_ref[0] = _layer_norm(r, g_ref[...], b_ref[...])


def _mla_out(ot, gate, h, wo, g, b, tm):
    bx, lx, _ = h.shape
    tile = lambda b_, i: (b_, i, 0)
    const2 = lambda b_, i: (0, 0)
    return pl.pallas_call(
        _mla_out_kernel,
        grid=(bx, lx // tm),
        in_specs=[
            pl.BlockSpec((1, MLA_WIDTH, tm), lambda b_, i: (b_, 0, i)),
            pl.BlockSpec((1, tm, MLA_WIDTH), tile),
            pl.BlockSpec((1, tm, D_MODEL), tile),
            pl.BlockSpec((MLA_WIDTH, D_MODEL), const2),
            pl.BlockSpec((1, D_MODEL), const2),
            pl.BlockSpec((1, D_MODEL), const2),
        ],
        out_specs=pl.BlockSpec((1, tm, D_MODEL), tile),
        out_shape=jax.ShapeDtypeStruct(h.shape, jnp.float32),
        compiler_params=pltpu.CompilerParams(
            dimension_semantics=("parallel", "parallel"), vmem_limit_bytes=VMEM_LIMIT),
        name="mla_out",
    )(ot, gate, h, wo, g, b)


def _conv_kernel(h_ref, carry_ref, win_ref, cw_ref, wout_ref, g_ref, b_ref,
                 out_ref, tail_ref, cu_ref, *, tail_start):
    tm = h_ref.shape[1]
    w = CONV_WIDTH

    @pl.when(pl.program_id(1) == 0)
    def _():
        cu_ref[0:SUBLANES, :] = carry_ref[0]

    h = h_ref[0]
    hb = _bf16(h)
    cu = _dot(hb, win_ref[:, w:2 * w]) * _dot(hb, win_ref[:, 2 * w:3 * w])
    cu_ref[SUBLANES:SUBLANES + tm, :] = cu
    conv = cu_ref[SUBLANES - 2:SUBLANES - 2 + tm, :] * cw_ref[0:1, :]
    conv = conv + cu_ref[SUBLANES - 1:SUBLANES - 1 + tm, :] * cw_ref[1:2, :]
    conv = conv + cu * cw_ref[2:3, :]
    tail_ref[0] = cu_ref[SUBLANES + tail_start:2 * SUBLANES + tail_start, :]
    cu_ref[0:SUBLANES, :] = cu_ref[tm:tm + SUBLANES, :]

    y = _dot(hb, win_ref[:, 0:w]) * conv * _silu(_dot(hb, win_ref[:, 3 * w:4 * w]))
    r = DN_ALPHA * h + _dot(_bf16(y), wout_ref[...])
    out_ref[0] = _layer_norm(r, g_ref[...], b_ref[...])


def _conv_layer(h, carry, win, cw, wout, g, b, tm, tail_start):
    bx, lx, _ = h.shape
    tile = lambda b_, i: (b_, i, 0)
    const2 = lambda b_, i: (0, 0)
    return pl.pallas_call(
        functools.partial(_conv_kernel, tail_start=tail_start),
        grid=(bx, lx // tm),
        in_specs=[
            pl.BlockSpec((1, tm, D_MODEL), tile),
            pl.BlockSpec((1, SUBLANES, CONV_WIDTH), lambda b_, i: (0, 0, 0)),
            pl.BlockSpec((D_MODEL, 4 * CONV_WIDTH), const2),
            pl.BlockSpec((SUBLANES, CONV_WIDTH), const2),
            pl.BlockSpec((CONV_WIDTH, D_MODEL), const2),
            pl.BlockSpec((1, D_MODEL), const2),
            pl.BlockSpec((1, D_MODEL), const2),
        ],
        out_specs=[
            pl.BlockSpec((1, tm, D_MODEL), tile),
            pl.BlockSpec((1, SUBLANES, CONV_WIDTH), lambda b_, i: (b_, 0, 0)),
        ],
        out_shape=[
            jax.ShapeDtypeStruct(h.shape, jnp.float32),
            jax.ShapeDtypeStruct((bx, SUBLANES, CONV_WIDTH), jnp.float32),
        ],
        scratch_shapes=[pltpu.VMEM((tm + SUBLANES, CONV_WIDTH), jnp.float32)],
        compiler_params=pltpu.CompilerParams(
            dimension_semantics=("arbitrary", "arbitrary"), vmem_limit_bytes=VMEM_LIMIT),
        name="conv_layer",
    )(h, carry, win, cw, wout, g, b)


def _prep_mla_weights(w_in, gq, w_uq, gkv, w_uk, w_uv, w_o):
    o_kr = Q_LORA + KV_LORA
    o_z = o_kr + QK_ROPE
    kr1 = w_in[:, o_kr:o_kr + HALF_ROPE]
    kr2 = w_in[:, o_kr + HALF_ROPE:o_z]
    zl = jnp.zeros((D_MODEL, QK_NOPE), w_in.dtype)
    zr = jnp.zeros((D_MODEL, HEAD_SLOT - QK_NOPE - QK_ROPE), w_in.dtype)
    w1 = jnp.concatenate([w_in[:, :o_kr], w_in[:, o_z:],
                          zl, kr1, kr2, zr, zl, -kr2, kr1, zr], axis=1)
    wuq = w_uq.reshape(Q_LORA, MLA_HEADS, QK_NOPE + QK_ROPE)
    wuq = jnp.pad(wuq, ((0, 0), (0, 0), (0, HEAD_SLOT - QK_NOPE - QK_ROPE)))
    wuk = jnp.pad(w_uk.reshape(KV_LORA, MLA_HEADS, QK_NOPE), ((0, 0), (0, 0), (0, HEAD_SLOT - QK_NOPE)))
    return {
        "w1": _bf16(w1),
        "gq": gq.reshape(1, Q_LORA),
        "gkv": gkv.reshape(1, KV_LORA),
        "wuq_t": _bf16(wuq.reshape(Q_LORA, QK_WIDTH).T),
        "wuk": _bf16(wuk.reshape(KV_LORA, QK_WIDTH)),
        "wuv_t": _bf16(w_uv.T),
        "wo": _bf16(w_o),
    }


def _rope_tables(pos):
    inv_freq = ROPE_BASE ** (-jnp.arange(0, QK_ROPE, 2, dtype=jnp.float32) / QK_ROPE)
    ang = pos[..., None] * inv_freq
    cos, sin = jnp.cos(ang), jnp.sin(ang)

    def slot(t):
        lead = ((0, 0),) * (t.ndim - 1)
        return jnp.pad(jnp.concatenate([t, t], axis=-1),
                       lead + ((QK_NOPE, HEAD_SLOT - QK_NOPE - QK_ROPE),))

    return jnp.swapaxes(cos, 1, 2), jnp.swapaxes(sin, 1, 2), slot(cos), slot(sin)


def kernel(x, positions, meta_tokens, ln_g, ln_b, mla_w_in, mla_q_norm_g, mla_w_uq, mla_kv_norm_g,
           mla_w_uk, mla_w_uv, mla_w_o, conv_w_in, conv_w, conv_w_out):
    bsz, seq, _ = x.shape
    assert seq % TOKEN_TILE == 0 and TOKEN_TILE % CHUNK == 0
    f32 = jnp.float32

    hf = x.astype(f32)
    hm = jnp.pad(meta_tokens.astype(f32), ((0, META_PAD - N_META), (0, 0)))[None]

    frame_tabs = _rope_tables((positions + N_META).astype(f32))
    meta_pos = jnp.pad(jnp.arange(N_META, dtype=f32), (0, META_PAD - N_META))[None]
    meta_tabs = _rope_tables(meta_pos)

    for i in range(DEPTH):
        j = i // 2
        g = ln_g[i].reshape(1, D_MODEL)
        b = ln_b[i].reshape(1, D_MODEL)
        if i % 2 == 0:
            w = _prep_mla_weights(mla_w_in[j], mla_q_norm_g[j], mla_w_uq[j], mla_kv_norm_g[j],
                                  mla_w_uk[j], mla_w_uv[j], mla_w_o[j])
            qt_m, k_m, vt_m, gate_m = _mla_proj(hm, *meta_tabs, w, META_PAD)
            qt_f, k_f, vt_f, gate_f = _mla_proj(hf, *frame_tabs, w, TOKEN_TILE)
            ot_m = _meta_attn(qt_m, k_m, vt_m)
            ot_f = _frame_attn(qt_f, k_f, vt_f, k_m, vt_m, TOKEN_TILE)
            hm = _mla_out(ot_m, gate_m, hm, w["wo"], g, b, META_PAD)
            hf = _mla_out(ot_f, gate_f, hf, w["wo"], g, b, TOKEN_TILE)
        else:
            win = _bf16(conv_w_in[j])
            wout = _bf16(conv_w_out[j])
            cw = jnp.pad(conv_w[j].astype(f32), ((0, SUBLANES - CONV_K), (0, 0)))
            zero_carry = jnp.zeros((1, SUBLANES, CONV_WIDTH), f32)
            hm, tail = _conv_layer(hm, zero_carry, win, cw, wout, g, b, META_PAD, N_META - SUBLANES)
            hf, _ = _conv_layer(hf, tail, win, cw, wout, g, b, TOKEN_TILE, 0)
    return hf
```

```python
import functools

import jax
import jax.numpy as jnp
from jax import lax
from jax.experimental import pallas as pl
from jax.experimental.pallas import tpu as pltpu

D_MODEL = 1024
DEPTH = 4
CHUNK = 64
CHUNK_SHIFT = CHUNK.bit_length() - 1
N_META = 16
MLA_HEADS = 16
QK_NOPE = 64
QK_ROPE = 32
HALF_ROPE = QK_ROPE // 2
V_HEAD = 64
Q_LORA = 384
KV_LORA = 256
MLA_WIDTH = MLA_HEADS * V_HEAD
ROPE_BASE = 10000.0
CONV_WIDTH = D_MODEL
CONV_K = 3
DN_ALPHA = (2 * DEPTH) ** 0.25
LN_EPS = 1e-5
RMS_EPS = 1e-6
NEG_INF = -1e30
LOG2_E = 1.4426950408889634

HEAD_SLOT = 128
QK_WIDTH = MLA_HEADS * HEAD_SLOT
MASK_LANE = QK_NOPE + QK_ROPE
BF16_ROWS = 16
V_SLOT = V_HEAD + BF16_ROWS
V_WIDTH = MLA_HEADS * V_SLOT
META_PAD = 128
TOKEN_TILE = 256
SUBLANES = 8
VMEM_LIMIT = 56 * 1024 * 1024

_OFF_CKV = Q_LORA
_OFF_Z = Q_LORA + KV_LORA
_OFF_RA = _OFF_Z + MLA_WIDTH
_OFF_RB = _OFF_RA + HEAD_SLOT
_W1_COLS = _OFF_RB + HEAD_SLOT

_NT = (((1,), (1,)), ((), ()))


def _bf16(x):
    return x.astype(jnp.bfloat16)


def _dot(a, b):
    return jnp.dot(a, b, preferred_element_type=jnp.float32)


def _dot_nt(a, b):
    return lax.dot_general(a, b, _NT, preferred_element_type=jnp.float32)


def _rms(x, g):
    y = x * lax.rsqrt(jnp.mean(jnp.square(x), axis=-1, keepdims=True) + RMS_EPS)
    return y * g


def _layer_norm(x, g, b):
    mu = jnp.mean(x, axis=-1, keepdims=True)
    xc = x - mu
    var = jnp.mean(jnp.square(xc), axis=-1, keepdims=True)
    return xc * lax.rsqrt(var + LN_EPS) * g + b


def _silu(z):
    return z * (1.0 / (1.0 + jnp.exp(-z)))


def _mla_proj_kernel(h_ref, cos_t_ref, sin_t_ref, cos_s_ref, sin_s_ref,
                     w1_ref, gq_ref, gkv_ref, wuq_ref, wuk_ref, wuv_ref,
                     qt_ref, k_ref, vt_ref, gate_ref, *, causal_lanes):
    hb = _bf16(h_ref[0])
    p = _dot(hb, w1_ref[...])
    cqn = _bf16(_rms(p[:, :Q_LORA], gq_ref[...]))
    ckvn = _bf16(_rms(p[:, _OFF_CKV:_OFF_Z], gkv_ref[...]))
    gate_ref[0] = _bf16(_silu(p[:, _OFF_Z:_OFF_RA]))

    kr_slot = p[:, _OFF_RA:_OFF_RB] * cos_s_ref[0] + p[:, _OFF_RB:_W1_COLS] * sin_s_ref[0]
    if causal_lanes:
        row = lax.broadcasted_iota(jnp.int32, kr_slot.shape, 0)
        c = lax.broadcasted_iota(jnp.int32, kr_slot.shape, 1) - MASK_LANE
        later_chunk = (c >= 0) & (c < kr_slot.shape[0] // CHUNK - 1) & (row >> CHUNK_SHIFT > c)
        kr_slot = jnp.where(later_chunk, 1.0, kr_slot)
    k_all = _dot(ckvn, wuk_ref[...])
    for hd in range(MLA_HEADS):
        k_ref[0, hd] = _bf16(k_all[:, hd * HEAD_SLOT:(hd + 1) * HEAD_SLOT] + kr_slot)

    vt = _dot_nt(wuv_ref[...], ckvn)
    ones_row = lax.broadcasted_iota(jnp.int32, (BF16_ROWS, vt.shape[1]), 0) == 0
    for hd in range(MLA_HEADS):
        vt_ref[0, 0, hd * V_SLOT:hd * V_SLOT + V_HEAD, :] = _bf16(vt[_head_rows(hd, V_HEAD)])
        vt_ref[0, 0, hd * V_SLOT + V_HEAD:(hd + 1) * V_SLOT, :] = ones_row.astype(jnp.bfloat16)

    scale = (QK_NOPE + QK_ROPE) ** -0.5 * LOG2_E
    qt = _dot_nt(wuq_ref[...], cqn) * scale
    cos_t = cos_t_ref[0]
    sin_t = sin_t_ref[0]
    for hd in range(MLA_HEADS):
        base = hd * HEAD_SLOT
        x1 = qt[base + QK_NOPE:base + QK_NOPE + HALF_ROPE]
        x2 = qt[base + QK_NOPE + HALF_ROPE:base + QK_NOPE + QK_ROPE]
        qt_ref[0, base:base + QK_NOPE, :] = _bf16(qt[base:base + QK_NOPE])
        qt_ref[0, base + QK_NOPE:base + QK_NOPE + HALF_ROPE, :] = _bf16(x1 * cos_t - x2 * sin_t)
        qt_ref[0, base + QK_NOPE + HALF_ROPE:base + QK_NOPE + QK_ROPE, :] = _bf16(x1 * sin_t + x2 * cos_t)
        qt_ref[0, base + QK_NOPE + QK_ROPE:base + HEAD_SLOT, :] = jnp.zeros(
            (HEAD_SLOT - QK_NOPE - QK_ROPE, qt.shape[1]), jnp.bfloat16)


def _mla_proj(h, cos_t, sin_t, cos_s, sin_s, w, tm, causal_lanes):
    bx, lx, _ = h.shape
    nt = lx // tm
    tile = lambda b, i: (b, i, 0)
    tile_t = lambda b, i: (b, 0, i)
    const2 = lambda b, i: (0, 0)
    return pl.pallas_call(
        functools.partial(_mla_proj_kernel, causal_lanes=causal_lanes),
        grid=(bx, nt),
        in_specs=[
            pl.BlockSpec((1, tm, D_MODEL), tile),
            pl.BlockSpec((1, HALF_ROPE, tm), tile_t),
            pl.BlockSpec((1, HALF_ROPE, tm), tile_t),
            pl.BlockSpec((1, tm, HEAD_SLOT), tile),
            pl.BlockSpec((1, tm, HEAD_SLOT), tile),
            pl.BlockSpec((D_MODEL, _W1_COLS), const2),
            pl.BlockSpec((1, Q_LORA), const2),
            pl.BlockSpec((1, KV_LORA), const2),
            pl.BlockSpec((QK_WIDTH, Q_LORA), const2),
            pl.BlockSpec((KV_LORA, QK_WIDTH), const2),
            pl.BlockSpec((MLA_WIDTH, KV_LORA), const2),
        ],
        out_specs=[
            pl.BlockSpec((1, QK_WIDTH, tm), tile_t),
            pl.BlockSpec((1, MLA_HEADS, tm, HEAD_SLOT), lambda b, i: (b, 0, i, 0)),
            pl.BlockSpec((1, 1, V_WIDTH, tm), lambda b, i: (b, i, 0, 0)),
            pl.BlockSpec((1, tm, MLA_WIDTH), tile),
        ],
        out_shape=[
            jax.ShapeDtypeStruct((bx, QK_WIDTH, lx), jnp.bfloat16),
            jax.ShapeDtypeStruct((bx, MLA_HEADS, lx, HEAD_SLOT), jnp.bfloat16),
            jax.ShapeDtypeStruct((bx, nt, V_WIDTH, tm), jnp.bfloat16),
            jax.ShapeDtypeStruct((bx, lx, MLA_WIDTH), jnp.bfloat16),
        ],
        compiler_params=pltpu.CompilerParams(
            dimension_semantics=("parallel", "parallel"), vmem_limit_bytes=VMEM_LIMIT),
        name="mla_proj",
    )(h, cos_t, sin_t, cos_s, sin_s, w["w1"], w["gq"], w["gkv"], w["wuq_t"], w["wuk"], w["wuv_t"])


def _softmax_start(s):
    m = jnp.max(s, axis=0, keepdims=True)
    return m, jnp.exp2(s - m)


def _normalize(acc):
    return acc[:V_HEAD] / acc[V_HEAD:V_HEAD + 1]


def _meta_scores(q, km):
    s = _dot(km, q)
    key = lax.broadcasted_iota(jnp.int32, s.shape, 0)
    return jnp.where(key < N_META, s, NEG_INF)


def _head_rows(hd, width):
    return slice(hd * width, (hd + 1) * width)


def _meta_attn_kernel(qt_ref, km_ref, vm_ref, ot_ref):
    for hd in range(MLA_HEADS):
        q = qt_ref[0, _head_rows(hd, HEAD_SLOT), :]
        _, p = _softmax_start(_meta_scores(q, km_ref[0, hd]))
        acc = _dot(vm_ref[0, 0, _head_rows(hd, V_SLOT), :], _bf16(p))
        ot_ref[0, _head_rows(hd, V_HEAD), :] = _bf16(_normalize(acc))


def _frame_attn_kernel(qt_ref, qn_ref, k_ref, vt_ref, km_ref, vm_ref, ot_ref,
                       s_ref, s0_ref, sm0_ref, m_ref, acc_ref):
    tk = vt_ref.shape[3]
    tq = qt_ref.shape[2]
    qi = pl.program_id(1)

    row = lax.broadcasted_iota(jnp.int32, (BF16_ROWS, tq), 0)
    qry = lax.broadcasted_iota(jnp.int32, (BF16_ROWS, tq), 1)
    hide = (qry >> CHUNK_SHIFT == row) & (row < tk // CHUNK - 1)

    def mask_rows(is_diagonal):
        return _bf16(jnp.where(hide & is_diagonal, NEG_INF, 0.0))

    def scores(q_ref, hd, j, mask):
        q = q_ref[0, _head_rows(hd, HEAD_SLOT), :]
        if mask is not None:
            q = jnp.concatenate([q[:MASK_LANE], mask, q[MASK_LANE + BF16_ROWS:]], axis=0)
        row0 = pl.multiple_of(j * tk, tk)
        return _dot(k_ref[0, hd, pl.ds(row0, tk), :], q)

    def update(hd, s, j):
        m = m_ref[hd]
        m_new = jnp.maximum(m, jnp.max(s, axis=0, keepdims=True))
        alpha = jnp.exp2(m - m_new)
        p = jnp.exp2(s - m_new)
        m_ref[hd] = m_new
        acc_ref[hd] = alpha * acc_ref[hd] + _dot(vt_ref[0, j, _head_rows(hd, V_SLOT), :], _bf16(p))

    def meta_scores():
        return [_dot(km_ref[0, hd], qt_ref[0, _head_rows(hd, HEAD_SLOT), :])
                for hd in range(MLA_HEADS)]

    def meta_values(hd, p):
        no_weight = jnp.zeros((META_PAD - N_META, tq), jnp.bfloat16)
        return _dot(vm_ref[0, 0, _head_rows(hd, V_SLOT), :],
                    jnp.concatenate([_bf16(p), no_weight], axis=0))

    @pl.when(qi == 0)
    def _():
        sm = meta_scores()
        diag_mask = mask_rows(True)
        for hd in range(MLA_HEADS):
            s_ref[0, hd] = scores(qt_ref, hd, 0, diag_mask)
            m, p = _softmax_start(sm[hd])
            m_ref[hd] = m
            acc_ref[hd] = meta_values(hd, p)

    @pl.when(qi > 0)
    def _():
        first_mask = mask_rows(qi == 1)
        for hd in range(MLA_HEADS):
            s_ref[1, hd] = scores(qt_ref, hd, 1, first_mask)
            sm = sm0_ref[hd]
            s = s0_ref[hd]
            m = jnp.maximum(jnp.max(sm, axis=0, keepdims=True), jnp.max(s, axis=0, keepdims=True))
            m_ref[hd] = m
            acc_ref[hd] = meta_values(hd, jnp.exp2(sm - m)) + _dot(
                vt_ref[0, 0, _head_rows(hd, V_SLOT), :], _bf16(jnp.exp2(s - m)))

    def pair_step(jj, carry):
        j = 2 * jj + 1
        for hd in range(MLA_HEADS):
            s_ref[0, hd] = scores(qt_ref, hd, j + 1, None)
            update(hd, s_ref[1, hd], j)
        last_mask = mask_rows(j + 2 == qi)
        for hd in range(MLA_HEADS):
            s_ref[1, hd] = scores(qt_ref, hd, j + 2, last_mask)
            update(hd, s_ref[0, hd], j + 1)
        return carry

    lax.fori_loop(0, (qi - 1) // 2, pair_step, 0)

    @pl.when((qi >= 2) & (qi % 2 == 0))
    def _():
        diag_mask = mask_rows(True)
        for hd in range(MLA_HEADS):
            s_ref[0, hd] = scores(qt_ref, hd, qi, diag_mask)
            update(hd, s_ref[1, hd], qi - 1)

    for hd in range(MLA_HEADS):
        keys = jnp.concatenate([k_ref[0, hd, 0:tk, :], km_ref[0, hd]], axis=0)
        s_next = _dot(keys, qn_ref[0, _head_rows(hd, HEAD_SLOT), :])
        s0_ref[hd] = s_next[:tk]
        sm0_ref[hd] = s_next[tk:]
        update(hd, s_ref[qi % 2, hd], qi)
    for hd in range(MLA_HEADS):
        ot_ref[0, _head_rows(hd, V_HEAD), :] = _bf16(_normalize(acc_ref[hd]))


def _meta_attn(qt, k, vt):
    return pl.pallas_call(
        _meta_attn_kernel,
        grid=(1,),
        in_specs=[
            pl.BlockSpec((1, QK_WIDTH, META_PAD), lambda i: (0, 0, 0)),
            pl.BlockSpec((1, MLA_HEADS, META_PAD, HEAD_SLOT), lambda i: (0, 0, 0, 0)),
            pl.BlockSpec((1, 1, V_WIDTH, META_PAD), lambda i: (0, 0, 0, 0)),
        ],
        out_specs=pl.BlockSpec((1, MLA_WIDTH, META_PAD), lambda i: (0, 0, 0)),
        out_shape=jax.ShapeDtypeStruct((1, MLA_WIDTH, META_PAD), jnp.bfloat16),
        name="meta_attn",
    )(qt, k, vt)


def _frame_attn(qt, k, vt, k_meta, vt_meta, tq):
    bx, _, lx = qt.shape
    nk, tk = vt.shape[1], vt.shape[3]
    assert tq == tk
    once = pl.Buffered(1)
    nq = lx // tq
    return pl.pallas_call(
        _frame_attn_kernel,
        grid=(bx, nq),
        in_specs=[
            pl.BlockSpec((1, QK_WIDTH, tq), lambda b, i: (b, 0, i)),
            pl.BlockSpec((1, QK_WIDTH, tq), lambda b, i: (b, 0, jnp.minimum(i + 1, nq - 1))),
            pl.BlockSpec((1, MLA_HEADS, lx, HEAD_SLOT), lambda b, i: (b, 0, 0, 0), pipeline_mode=once),
            pl.BlockSpec((1, nk, V_WIDTH, tk), lambda b, i: (b, 0, 0, 0), pipeline_mode=once),
            pl.BlockSpec((1, MLA_HEADS, N_META, HEAD_SLOT), lambda b, i: (0, 0, 0, 0)),
            pl.BlockSpec((1, 1, V_WIDTH, META_PAD), lambda b, i: (0, 0, 0, 0)),
        ],
        out_specs=pl.BlockSpec((1, MLA_WIDTH, tq), lambda b, i: (b, 0, i)),
        out_shape=jax.ShapeDtypeStruct((bx, MLA_WIDTH, lx), jnp.bfloat16),
        scratch_shapes=[
            pltpu.VMEM((2, MLA_HEADS, tk, tq), jnp.float32),
            pltpu.VMEM((MLA_HEADS, tk, tq), jnp.float32),
            pltpu.VMEM((MLA_HEADS, N_META, tq), jnp.float32),
            pltpu.VMEM((MLA_HEADS, 1, tq), jnp.float32),
            pltpu.VMEM((MLA_HEADS, V_SLOT, tq), jnp.float32),
        ],
        compiler_params=pltpu.CompilerParams(
            dimension_semantics=("parallel", "arbitrary"), vmem_limit_bytes=VMEM_LIMIT),
        name="frame_attn",
    )(qt, qt, k, vt, k_meta, vt_meta)


def _mla_out_kernel(ot_ref, gate_ref, h_ref, wo_ref, g_ref, b_ref, out_ref):
    o = ot_ref[0].astype(jnp.float32).T
    y = _bf16(o * gate_ref[0].astype(jnp.float32))
    r = DN_ALPHA * h_ref[0] + _dot(y, wo_ref[...])
    out_ref[0] = _layer_norm(r, g_ref[...], b_ref[...])


def _mla_out(ot, gate, h, wo, g, b, tm):
    bx, lx, _ = h.shape
    tile = lambda b_, i: (b_, i, 0)
    const2 = lambda b_, i: (0, 0)
    return pl.pallas_call(
        _mla_out_kernel,
        grid=(bx, lx // tm),
        in_specs=[
            pl.BlockSpec((1, MLA_WIDTH, tm), lambda b_, i: (b_, 0, i)),
            pl.BlockSpec((1, tm, MLA_WIDTH), tile),
            pl.BlockSpec((1, tm, D_MODEL), tile),
            pl.BlockSpec((MLA_WIDTH, D_MODEL), const2),
            pl.BlockSpec((1, D_MODEL), const2),
            pl.BlockSpec((1, D_MODEL), const2),
        ],
        out_specs=pl.BlockSpec((1, tm, D_MODEL), tile),
        out_shape=jax.ShapeDtypeStruct(h.shape, jnp.float32),
        compiler_params=pltpu.CompilerParams(
            dimension_semantics=("parallel", "parallel"), vmem_limit_bytes=VMEM_LIMIT),
        name="mla_out",
    )(ot, gate, h, wo, g, b)


def _conv_kernel(h_ref, carry_ref, win_ref, cw_ref, wout_ref, g_ref, b_ref,
                 out_ref, tail_ref, cu_ref, *, tail_start):
    tm = h_ref.shape[1]
    w = CONV_WIDTH

    @pl.when(pl.program_id(1) == 0)
    def _():
        cu_ref[0:SUBLANES, :] = carry_ref[0]

    h = h_ref[0]
    hb = _bf16(h)
    cu = _dot(hb, win_ref[:, w:2 * w]) * _dot(hb, win_ref[:, 2 * w:3 * w])
    cu_ref[SUBLANES:SUBLANES + tm, :] = cu
    conv = cu_ref[SUBLANES - 2:SUBLANES - 2 + tm, :] * cw_ref[0:1, :]
    conv = conv + cu_ref[SUBLANES - 1:SUBLANES - 1 + tm, :] * cw_ref[1:2, :]
    conv = conv + cu * cw_ref[2:3, :]
    tail_ref[0] = cu_ref[SUBLANES + tail_start:2 * SUBLANES + tail_start, :]
    cu_ref[0:SUBLANES, :] = cu_ref[tm:tm + SUBLANES, :]

    y = _dot(hb, win_ref[:, 0:w]) * conv * _silu(_dot(hb, win_ref[:, 3 * w:4 * w]))
    r = DN_ALPHA * h + _dot(_bf16(y), wout_ref[...])
    out_ref[0] = _layer_norm(r, g_ref[...], b_ref[...])


def _conv_layer(h, carry, win, cw, wout, g, b, tm, tail_start):
    bx, lx, _ = h.shape
    tile = lambda b_, i: (b_, i, 0)
    const2 = lambda b_, i: (0, 0)
    return pl.pallas_call(
        functools.partial(_conv_kernel, tail_start=tail_start),
        grid=(bx, lx // tm),
        in_specs=[
            pl.BlockSpec((1, tm, D_MODEL), tile),
            pl.BlockSpec((1, SUBLANES, CONV_WIDTH), lambda b_, i: (0, 0, 0)),
            pl.BlockSpec((D_MODEL, 4 * CONV_WIDTH), const2),
            pl.BlockSpec((SUBLANES, CONV_WIDTH), const2),
            pl.BlockSpec((CONV_WIDTH, D_MODEL), const2),
            pl.BlockSpec((1, D_MODEL), const2),
            pl.BlockSpec((1, D_MODEL), const2),
        ],
        out_specs=[
            pl.BlockSpec((1, tm, D_MODEL), tile),
            pl.BlockSpec((1, SUBLANES, CONV_WIDTH), lambda b_, i: (b_, 0, 0)),
        ],
        out_shape=[
            jax.ShapeDtypeStruct(h.shape, jnp.float32),
            jax.ShapeDtypeStruct((bx, SUBLANES, CONV_WIDTH), jnp.float32),
        ],
        scratch_shapes=[pltpu.VMEM((tm + SUBLANES, CONV_WIDTH), jnp.float32)],
        compiler_params=pltpu.CompilerParams(
            dimension_semantics=("arbitrary", "arbitrary"), vmem_limit_bytes=VMEM_LIMIT),
        name="conv_layer",
    )(h, carry, win, cw, wout, g, b)


def _prep_mla_weights(w_in, gq, w_uq, gkv, w_uk, w_uv, w_o):
    o_kr = Q_LORA + KV_LORA
    o_z = o_kr + QK_ROPE
    kr1 = w_in[:, o_kr:o_kr + HALF_ROPE]
    kr2 = w_in[:, o_kr + HALF_ROPE:o_z]
    zl = jnp.zeros((D_MODEL, QK_NOPE), w_in.dtype)
    zr = jnp.zeros((D_MODEL, HEAD_SLOT - QK_NOPE - QK_ROPE), w_in.dtype)
    w1 = jnp.concatenate([w_in[:, :o_kr], w_in[:, o_z:],
                          zl, kr1, kr2, zr, zl, -kr2, kr1, zr], axis=1)
    wuq = w_uq.reshape(Q_LORA, MLA_HEADS, QK_NOPE + QK_ROPE)
    wuq = jnp.pad(wuq, ((0, 0), (0, 0), (0, HEAD_SLOT - QK_NOPE - QK_ROPE)))
    wuk = jnp.pad(w_uk.reshape(KV_LORA, MLA_HEADS, QK_NOPE), ((0, 0), (0, 0), (0, HEAD_SLOT - QK_NOPE)))
    return {
        "w1": _bf16(w1),
        "gq": gq.reshape(1, Q_LORA),
        "gkv": gkv.reshape(1, KV_LORA),
        "wuq_t": _bf16(wuq.reshape(Q_LORA, QK_WIDTH).T),
        "wuk": _bf16(wuk.reshape(KV_LORA, QK_WIDTH)),
        "wuv_t": _bf16(w_uv.T),
        "wo": _bf16(w_o),
    }


def _rope_tables(pos):
    inv_freq = ROPE_BASE ** (-jnp.arange(0, QK_ROPE, 2, dtype=jnp.float32) / QK_ROPE)
    ang = pos[..., None] * inv_freq
    cos, sin = jnp.cos(ang), jnp.sin(ang)

    def slot(t):
        lead = ((0, 0),) * (t.ndim - 1)
        return jnp.pad(jnp.concatenate([t, t], axis=-1),
                       lead + ((QK_NOPE, HEAD_SLOT - QK_NOPE - QK_ROPE),))

    return jnp.swapaxes(cos, 1, 2), jnp.swapaxes(sin, 1, 2), slot(cos), slot(sin)


def kernel(x, positions, meta_tokens, ln_g, ln_b, mla_w_in, mla_q_norm_g, mla_w_uq, mla_kv_norm_g,
           mla_w_uk, mla_w_uv, mla_w_o, conv_w_in, conv_w, conv_w_out):
    bsz, seq, _ = x.shape
    assert seq % TOKEN_TILE == 0 and TOKEN_TILE % CHUNK == 0
    f32 = jnp.float32

    hf = x.astype(f32)
    hm = jnp.pad(meta_tokens.astype(f32), ((0, META_PAD - N_META), (0, 0)))[None]

    frame_tabs = _rope_tables((positions + N_META).astype(f32))
    meta_pos = jnp.pad(jnp.arange(N_META, dtype=f32), (0, META_PAD - N_META))[None]
    meta_tabs = _rope_tables(meta_pos)

    for i in range(DEPTH):
        j = i // 2
        g = ln_g[i].reshape(1, D_MODEL)
        b = ln_b[i].reshape(1, D_MODEL)
        if i % 2 == 0:
            w = _prep_mla_weights(mla_w_in[j], mla_q_norm_g[j], mla_w_uq[j], mla_kv_norm_g[j],
                                  mla_w_uk[j], mla_w_uv[j], mla_w_o[j])
            qt_m, k_m, vt_m, gate_m = _mla_proj(hm, *meta_tabs, w, META_PAD, causal_lanes=False)
            qt_f, k_f, vt_f, gate_f = _mla_proj(hf, *frame_tabs, w, TOKEN_TILE, causal_lanes=True)
            ot_m = _meta_attn(qt_m, k_m, vt_m)
            ot_f = _frame_attn(qt_f, k_f, vt_f, k_m, vt_m, TOKEN_TILE)
            hm = _mla_out(ot_m, gate_m, hm, w["wo"], g, b, META_PAD)
            hf = _mla_out(ot_f, gate_f, hf, w["wo"], g, b, TOKEN_TILE)
        else:
            win = _bf16(conv_w_in[j])
            wout = _bf16(conv_w_out[j])
            cw = jnp.pad(conv_w[j].astype(f32), ((0, SUBLANES - CONV_K), (0, 0)))
            zero_carry = jnp.zeros((1, SUBLANES, CONV_WIDTH), f32)
            hm, tail = _conv_layer(hm, zero_carry, win, cw, wout, g, b, META_PAD, N_META - SUBLANES)
            hf, _ = _conv_layer(hf, tail, win, cw, wout, g, b, TOKEN_TILE, 0)
    return hf
```

```python
import functools

import jax
import jax.numpy as jnp
from jax import lax
from jax.experimental import pallas as pl
from jax.experimental.pallas import tpu as pltpu

D_MODEL = 1024
DEPTH = 4
CHUNK = 64
CHUNK_SHIFT = CHUNK.bit_length() - 1
N_META = 16
MLA_HEADS = 16
QK_NOPE = 64
QK_ROPE = 32
HALF_ROPE = QK_ROPE // 2
V_HEAD = 64
Q_LORA = 384
KV_LORA = 256
MLA_WIDTH = MLA_HEADS * V_HEAD
ROPE_BASE = 10000.0
CONV_WIDTH = D_MODEL
CONV_K = 3
DN_ALPHA = (2 * DEPTH) ** 0.25
LN_EPS = 1e-5
RMS_EPS = 1e-6
NEG_INF = -1e30
LOG2_E = 1.4426950408889634

HEAD_SLOT = 128
QK_WIDTH = MLA_HEADS * HEAD_SLOT
MASK_LANE = QK_NOPE + QK_ROPE
BF16_ROWS = 16
V_SLOT = V_HEAD + BF16_ROWS
V_WIDTH = MLA_HEADS * V_SLOT
META_PAD = 128
TOKEN_TILE = 256
CONV_TILE = 512
SUBLANES = 8
VMEM_LIMIT = 56 * 1024 * 1024

_OFF_CKV = Q_LORA
_OFF_Z = Q_LORA + KV_LORA
_OFF_RA = _OFF_Z + MLA_WIDTH
_OFF_RB = _OFF_RA + HEAD_SLOT
_W1_COLS = _OFF_RB + HEAD_SLOT

_NT = (((1,), (1,)), ((), ()))


def _bf16(x):
    return x.astype(jnp.bfloat16)


def _dot(a, b):
    return jnp.dot(a, b, preferred_element_type=jnp.float32)


def _dot_nt(a, b):
    return lax.dot_general(a, b, _NT, preferred_element_type=jnp.float32)


def _rms(x, g):
    y = x * lax.rsqrt(jnp.mean(jnp.square(x), axis=-1, keepdims=True) + RMS_EPS)
    return y * g


def _layer_norm(x, g, b):
    mu = jnp.mean(x, axis=-1, keepdims=True)
    xc = x - mu
    var = jnp.mean(jnp.square(xc), axis=-1, keepdims=True)
    return xc * lax.rsqrt(var + LN_EPS) * g + b


def _silu(z):
    return z * (1.0 / (1.0 + jnp.exp(-z)))


def _mla_proj_kernel(h_ref, cos_t_ref, sin_t_ref, cos_s_ref, sin_s_ref,
                     w1_ref, gq_ref, gkv_ref, wuq_ref, wuk_ref, wuv_ref,
                     qt_ref, k_ref, vt_ref, gate_ref, *, causal_lanes):
    hb = _bf16(h_ref[0])
    p = _dot(hb, w1_ref[...])
    cqn = _bf16(_rms(p[:, :Q_LORA], gq_ref[...]))
    ckvn = _bf16(_rms(p[:, _OFF_CKV:_OFF_Z], gkv_ref[...]))
    gate_ref[0] = _bf16(_silu(p[:, _OFF_Z:_OFF_RA]))

    kr_slot = p[:, _OFF_RA:_OFF_RB] * cos_s_ref[0] + p[:, _OFF_RB:_W1_COLS] * sin_s_ref[0]
    if causal_lanes:
        row = lax.broadcasted_iota(jnp.int32, kr_slot.shape, 0)
        c = lax.broadcasted_iota(jnp.int32, kr_slot.shape, 1) - MASK_LANE
        later_chunk = (c >= 0) & (c < kr_slot.shape[0] // CHUNK - 1) & (row >> CHUNK_SHIFT > c)
        kr_slot = jnp.where(later_chunk, 1.0, kr_slot)
    k_all = _dot(ckvn, wuk_ref[...])
    for hd in range(MLA_HEADS):
        k_ref[0, hd] = _bf16(k_all[:, hd * HEAD_SLOT:(hd + 1) * HEAD_SLOT] + kr_slot)

    vt = _dot_nt(wuv_ref[...], ckvn)
    ones_row = lax.broadcasted_iota(jnp.int32, (BF16_ROWS, vt.shape[1]), 0) == 0
    for hd in range(MLA_HEADS):
        vt_ref[0, 0, hd * V_SLOT:hd * V_SLOT + V_HEAD, :] = _bf16(vt[_head_rows(hd, V_HEAD)])
        vt_ref[0, 0, hd * V_SLOT + V_HEAD:(hd + 1) * V_SLOT, :] = ones_row.astype(jnp.bfloat16)

    scale = (QK_NOPE + QK_ROPE) ** -0.5 * LOG2_E
    qt = _dot_nt(wuq_ref[...], cqn) * scale
    cos_t = cos_t_ref[0]
    sin_t = sin_t_ref[0]
    for hd in range(MLA_HEADS):
        base = hd * HEAD_SLOT
        x1 = qt[base + QK_NOPE:base + QK_NOPE + HALF_ROPE]
        x2 = qt[base + QK_NOPE + HALF_ROPE:base + QK_NOPE + QK_ROPE]
        qt_ref[0, base:base + QK_NOPE, :] = _bf16(qt[base:base + QK_NOPE])
        qt_ref[0, base + QK_NOPE:base + QK_NOPE + HALF_ROPE, :] = _bf16(x1 * cos_t - x2 * sin_t)
        qt_ref[0, base + QK_NOPE + HALF_ROPE:base + QK_NOPE + QK_ROPE, :] = _bf16(x1 * sin_t + x2 * cos_t)
        qt_ref[0, base + QK_NOPE + QK_ROPE:base + HEAD_SLOT, :] = jnp.zeros(
            (HEAD_SLOT - QK_NOPE - QK_ROPE, qt.shape[1]), jnp.bfloat16)


def _mla_proj(h, cos_t, sin_t, cos_s, sin_s, w, tm, causal_lanes):
    bx, lx, _ = h.shape
    nt = lx // tm
    tile = lambda b, i: (b, i, 0)
    tile_t = lambda b, i: (b, 0, i)
    const2 = lambda b, i: (0, 0)
    return pl.pallas_call(
        functools.partial(_mla_proj_kernel, causal_lanes=causal_lanes),
        grid=(bx, nt),
        in_specs=[
            pl.BlockSpec((1, tm, D_MODEL), tile),
            pl.BlockSpec((1, HALF_ROPE, tm), tile_t),
            pl.BlockSpec((1, HALF_ROPE, tm), tile_t),
            pl.BlockSpec((1, tm, HEAD_SLOT), tile),
            pl.BlockSpec((1, tm, HEAD_SLOT), tile),
            pl.BlockSpec((D_MODEL, _W1_COLS), const2),
            pl.BlockSpec((1, Q_LORA), const2),
            pl.BlockSpec((1, KV_LORA), const2),
            pl.BlockSpec((QK_WIDTH, Q_LORA), const2),
            pl.BlockSpec((KV_LORA, QK_WIDTH), const2),
            pl.BlockSpec((MLA_WIDTH, KV_LORA), const2),
        ],
        out_specs=[
            pl.BlockSpec((1, QK_WIDTH, tm), tile_t),
            pl.BlockSpec((1, MLA_HEADS, tm, HEAD_SLOT), lambda b, i: (b, 0, i, 0)),
            pl.BlockSpec((1, 1, V_WIDTH, tm), lambda b, i: (b, i, 0, 0)),
            pl.BlockSpec((1, tm, MLA_WIDTH), tile),
        ],
        out_shape=[
            jax.ShapeDtypeStruct((bx, QK_WIDTH, lx), jnp.bfloat16),
            jax.ShapeDtypeStruct((bx, MLA_HEADS, lx, HEAD_SLOT), jnp.bfloat16),
            jax.ShapeDtypeStruct((bx, nt, V_WIDTH, tm), jnp.bfloat16),
            jax.ShapeDtypeStruct((bx, lx, MLA_WIDTH), jnp.bfloat16),
        ],
        compiler_params=pltpu.CompilerParams(
            dimension_semantics=("parallel", "parallel"), vmem_limit_bytes=VMEM_LIMIT),
        name="mla_proj",
    )(h, cos_t, sin_t, cos_s, sin_s, w["w1"], w["gq"], w["gkv"], w["wuq_t"], w["wuk"], w["wuv_t"])


def _softmax_start(s):
    m = jnp.max(s, axis=0, keepdims=True)
    return m, jnp.exp2(s - m)


def _normalize(acc):
    return acc[:V_HEAD] / acc[V_HEAD:V_HEAD + 1]


def _meta_scores(q, km):
    s = _dot(km, q)
    key = lax.broadcasted_iota(jnp.int32, s.shape, 0)
    return jnp.where(key < N_META, s, NEG_INF)


def _head_rows(hd, width):
    return slice(hd * width, (hd + 1) * width)


def _meta_attn_kernel(qt_ref, km_ref, vm_ref, ot_ref):
    for hd in range(MLA_HEADS):
        q = qt_ref[0, _head_rows(hd, HEAD_SLOT), :]
        _, p = _softmax_start(_meta_scores(q, km_ref[0, hd]))
        acc = _dot(vm_ref[0, 0, _head_rows(hd, V_SLOT), :], _bf16(p))
        ot_ref[0, _head_rows(hd, V_HEAD), :] = _bf16(_normalize(acc))


def _frame_attn_kernel(qt_ref, qn_ref, k_ref, vt_ref, km_ref, vm_ref, ot_ref,
                       s_ref, s0_ref, sm0_ref, m_ref, acc_ref):
    tk = vt_ref.shape[3]
    tq = qt_ref.shape[2]
    qi = pl.program_id(1)

    row = lax.broadcasted_iota(jnp.int32, (BF16_ROWS, tq), 0)
    qry = lax.broadcasted_iota(jnp.int32, (BF16_ROWS, tq), 1)
    hide = (qry >> CHUNK_SHIFT == row) & (row < tk // CHUNK - 1)

    def mask_rows(is_diagonal):
        return _bf16(jnp.where(hide & is_diagonal, NEG_INF, 0.0))

    def scores(q_ref, hd, j, mask):
        q = q_ref[0, _head_rows(hd, HEAD_SLOT), :]
        if mask is not None:
            q = jnp.concatenate([q[:MASK_LANE], mask, q[MASK_LANE + BF16_ROWS:]], axis=0)
        row0 = pl.multiple_of(j * tk, tk)
        return _dot(k_ref[0, hd, pl.ds(row0, tk), :], q)

    def update(hd, s, j):
        m = m_ref[hd]
        m_new = jnp.maximum(m, jnp.max(s, axis=0, keepdims=True))
        alpha = jnp.exp2(m - m_new)
        p = jnp.exp2(s - m_new)
        m_ref[hd] = m_new
        acc_ref[hd] = alpha * acc_ref[hd] + _dot(vt_ref[0, j, _head_rows(hd, V_SLOT), :], _bf16(p))

    def meta_scores():
        return [_dot(km_ref[0, hd], qt_ref[0, _head_rows(hd, HEAD_SLOT), :])
                for hd in range(MLA_HEADS)]

    def meta_values(hd, p):
        no_weight = jnp.zeros((META_PAD - N_META, tq), jnp.bfloat16)
        return _dot(vm_ref[0, 0, _head_rows(hd, V_SLOT), :],
                    jnp.concatenate([_bf16(p), no_weight], axis=0))

    @pl.when(qi == 0)
    def _():
        sm = meta_scores()
        diag_mask = mask_rows(True)
        for hd in range(MLA_HEADS):
            s_ref[0, hd] = scores(qt_ref, hd, 0, diag_mask)
            m, p = _softmax_start(sm[hd])
            m_ref[hd] = m
            acc_ref[hd] = meta_values(hd, p)

    @pl.when(qi > 0)
    def _():
        first_mask = mask_rows(qi == 1)
        for hd in range(MLA_HEADS):
            s_ref[1, hd] = scores(qt_ref, hd, 1, first_mask)
            sm = sm0_ref[hd]
            s = s0_ref[hd]
            m = jnp.maximum(jnp.max(sm, axis=0, keepdims=True), jnp.max(s, axis=0, keepdims=True))
            m_ref[hd] = m
            acc_ref[hd] = meta_values(hd, jnp.exp2(sm - m)) + _dot(
                vt_ref[0, 0, _head_rows(hd, V_SLOT), :], _bf16(jnp.exp2(s - m)))

    def pair_step(jj, carry):
        j = 2 * jj + 1
        for hd in range(MLA_HEADS):
            s_ref[0, hd] = scores(qt_ref, hd, j + 1, None)
            update(hd, s_ref[1, hd], j)
        last_mask = mask_rows(j + 2 == qi)
        for hd in range(MLA_HEADS):
            s_ref[1, hd] = scores(qt_ref, hd, j + 2, last_mask)
            update(hd, s_ref[0, hd], j + 1)
        return carry

    lax.fori_loop(0, (qi - 1) // 2, pair_step, 0)

    @pl.when((qi >= 2) & (qi % 2 == 0))
    def _():
        diag_mask = mask_rows(True)
        for hd in range(MLA_HEADS):
            s_ref[0, hd] = scores(qt_ref, hd, qi, diag_mask)
            update(hd, s_ref[1, hd], qi - 1)

    for hd in range(MLA_HEADS):
        keys = jnp.concatenate([k_ref[0, hd, 0:tk, :], km_ref[0, hd]], axis=0)
        s_next = _dot(keys, qn_ref[0, _head_rows(hd, HEAD_SLOT), :])
        s0_ref[hd] = s_next[:tk]
        sm0_ref[hd] = s_next[tk:]
        update(hd, s_ref[qi % 2, hd], qi)
    for hd in range(MLA_HEADS):
        ot_ref[0, _head_rows(hd, V_HEAD), :] = _bf16(_normalize(acc_ref[hd]))


def _meta_attn(qt, k, vt):
    return pl.pallas_call(
        _meta_attn_kernel,
        grid=(1,),
        in_specs=[
            pl.BlockSpec((1, QK_WIDTH, META_PAD), lambda i: (0, 0, 0)),
            pl.BlockSpec((1, MLA_HEADS, META_PAD, HEAD_SLOT), lambda i: (0, 0, 0, 0)),
            pl.BlockSpec((1, 1, V_WIDTH, META_PAD), lambda i: (0, 0, 0, 0)),
        ],
        out_specs=pl.BlockSpec((1, MLA_WIDTH, META_PAD), lambda i: (0, 0, 0)),
        out_shape=jax.ShapeDtypeStruct((1, MLA_WIDTH, META_PAD), jnp.bfloat16),
        name="meta_attn",
    )(qt, k, vt)


def _frame_attn(qt, k, vt, k_meta, vt_meta, tq):
    bx, _, lx = qt.shape
    nk, tk = vt.shape[1], vt.shape[3]
    assert tq == tk
    once = pl.Buffered(1)
    nq = lx // tq
    return pl.pallas_call(
        _frame_attn_kernel,
        grid=(bx, nq),
        in_specs=[
            pl.BlockSpec((1, QK_WIDTH, tq), lambda b, i: (b, 0, i)),
            pl.BlockSpec((1, QK_WIDTH, tq), lambda b, i: (b, 0, jnp.minimum(i + 1, nq - 1))),
            pl.BlockSpec((1, MLA_HEADS, lx, HEAD_SLOT), lambda b, i: (b, 0, 0, 0), pipeline_mode=once),
            pl.BlockSpec((1, nk, V_WIDTH, tk), lambda b, i: (b, 0, 0, 0), pipeline_mode=once),
            pl.BlockSpec((1, MLA_HEADS, N_META, HEAD_SLOT), lambda b, i: (0, 0, 0, 0)),
            pl.BlockSpec((1, 1, V_WIDTH, META_PAD), lambda b, i: (0, 0, 0, 0)),
        ],
        out_specs=pl.BlockSpec((1, MLA_WIDTH, tq), lambda b, i: (b, 0, i)),
        out_shape=jax.ShapeDtypeStruct((bx, MLA_WIDTH, lx), jnp.bfloat16),
        scratch_shapes=[
            pltpu.VMEM((2, MLA_HEADS, tk, tq), jnp.float32),
            pltpu.VMEM((MLA_HEADS, tk, tq), jnp.float32),
            pltpu.VMEM((MLA_HEADS, N_META, tq), jnp.float32),
            pltpu.VMEM((MLA_HEADS, 1, tq), jnp.float32),
            pltpu.VMEM((MLA_HEADS, V_SLOT, tq), jnp.float32),
        ],
        compiler_params=pltpu.CompilerParams(
            dimension_semantics=("parallel", "arbitrary"), vmem_limit_bytes=VMEM_LIMIT),
        name="frame_attn",
    )(qt, qt, k, vt, k_meta, vt_meta)


def _out_conv_kernel(ot_ref, gate_ref, h_ref, carry_ref, wo_ref, g1_ref, b1_ref,
                     win_ref, cw_ref, wout_ref, g2_ref, b2_ref,
                     out_ref, tail_ref, cu_ref, *, tail_start, chunk):
    tm = h_ref.shape[1]
    w = CONV_WIDTH

    @pl.when(pl.program_id(1) == 0)
    def _():
        cu_ref[0:SUBLANES, :] = carry_ref[0]

    chunks = [slice(c0, c0 + chunk) for c0 in range(0, tm, chunk)]

    attn = []
    for rows in chunks:
        o = ot_ref[0, :, rows].astype(jnp.float32).T
        y = _bf16(o * gate_ref[0, rows, :].astype(jnp.float32))
        attn.append(_dot(y, wo_ref[...]))

    hs, ys = [], []
    for rows, a in zip(chunks, attn):
        h = _layer_norm(DN_ALPHA * h_ref[0, rows, :] + a, g1_ref[...], b1_ref[...])
        hb = _bf16(h)
        cu = _dot(hb, win_ref[:, w:2 * w]) * _dot(hb, win_ref[:, 2 * w:3 * w])
        c0 = rows.start
        cu_ref[SUBLANES + c0:SUBLANES + c0 + chunk, :] = cu
        conv = cu_ref[SUBLANES - 2 + c0:SUBLANES - 2 + c0 + chunk, :] * cw_ref[0:1, :]
        conv = conv + cu_ref[SUBLANES - 1 + c0:SUBLANES - 1 + c0 + chunk, :] * cw_ref[1:2, :]
        conv = conv + cu * cw_ref[2:3, :]
        hs.append(h)
        ys.append(_dot(hb, win_ref[:, 0:w]) * conv * _silu(_dot(hb, win_ref[:, 3 * w:4 * w])))

    for rows, h, y in zip(chunks, hs, ys):
        r = DN_ALPHA * h + _dot(_bf16(y), wout_ref[...])
        out_ref[0, rows, :] = _layer_norm(r, g2_ref[...], b2_ref[...])
    tail_ref[0] = cu_ref[SUBLANES + tail_start:2 * SUBLANES + tail_start, :]
    cu_ref[0:SUBLANES, :] = cu_ref[tm:tm + SUBLANES, :]


def _out_conv(ot, gate, h, carry, w, tm, tail_start):
    bx, lx, _ = h.shape
    chunk = min(tm, TOKEN_TILE)
    tile = lambda b_, i: (b_, i, 0)
    once = pl.Buffered(1)
    const = lambda shape: pl.BlockSpec(shape, lambda b_, i: (0,) * len(shape), pipeline_mode=once)
    return pl.pallas_call(
        functools.partial(_out_conv_kernel, tail_start=tail_start, chunk=chunk),
        grid=(bx, lx // tm),
        in_specs=[
            pl.BlockSpec((1, MLA_WIDTH, tm), lambda b_, i: (b_, 0, i)),
            pl.BlockSpec((1, tm, MLA_WIDTH), tile),
            pl.BlockSpec((1, tm, D_MODEL), tile),
            const((1, SUBLANES, CONV_WIDTH)),
            const((MLA_WIDTH, D_MODEL)),
            const((1, D_MODEL)),
            const((1, D_MODEL)),
            const((D_MODEL, 4 * CONV_WIDTH)),
            const((SUBLANES, CONV_WIDTH)),
            const((CONV_WIDTH, D_MODEL)),
            const((1, D_MODEL)),
            const((1, D_MODEL)),
        ],
        out_specs=[
            pl.BlockSpec((1, tm, D_MODEL), tile),
            pl.BlockSpec((1, SUBLANES, CONV_WIDTH), lambda b_, i: (b_, 0, 0)),
        ],
        out_shape=[
            jax.ShapeDtypeStruct(h.shape, jnp.float32),
            jax.ShapeDtypeStruct((bx, SUBLANES, CONV_WIDTH), jnp.float32),
        ],
        scratch_shapes=[pltpu.VMEM((tm + SUBLANES, CONV_WIDTH), jnp.float32)],
        compiler_params=pltpu.CompilerParams(
            dimension_semantics=("arbitrary", "arbitrary"), vmem_limit_bytes=VMEM_LIMIT),
        name="out_conv",
    )(ot, gate, h, carry, w["wo"], w["g1"], w["b1"], w["win"], w["cw"], w["wout"], w["g2"], w["b2"])


def _prep_mla_weights(w_in, gq, w_uq, gkv, w_uk, w_uv, w_o):
    o_kr = Q_LORA + KV_LORA
    o_z = o_kr + QK_ROPE
    kr1 = w_in[:, o_kr:o_kr + HALF_ROPE]
    kr2 = w_in[:, o_kr + HALF_ROPE:o_z]
    zl = jnp.zeros((D_MODEL, QK_NOPE), w_in.dtype)
    zr = jnp.zeros((D_MODEL, HEAD_SLOT - QK_NOPE - QK_ROPE), w_in.dtype)
    w1 = jnp.concatenate([w_in[:, :o_kr], w_in[:, o_z:],
                          zl, kr1, kr2, zr, zl, -kr2, kr1, zr], axis=1)
    wuq = w_uq.reshape(Q_LORA, MLA_HEADS, QK_NOPE + QK_ROPE)
    wuq = jnp.pad(wuq, ((0, 0), (0, 0), (0, HEAD_SLOT - QK_NOPE - QK_ROPE)))
    wuk = jnp.pad(w_uk.reshape(KV_LORA, MLA_HEADS, QK_NOPE), ((0, 0), (0, 0), (0, HEAD_SLOT - QK_NOPE)))
    return {
        "w1": _bf16(w1),
        "gq": gq.reshape(1, Q_LORA),
        "gkv": gkv.reshape(1, KV_LORA),
        "wuq_t": _bf16(wuq.reshape(Q_LORA, QK_WIDTH).T),
        "wuk": _bf16(wuk.reshape(KV_LORA, QK_WIDTH)),
        "wuv_t": _bf16(w_uv.T),
        "wo": _bf16(w_o),
    }


def _rope_tables(pos):
    inv_freq = ROPE_BASE ** (-jnp.arange(0, QK_ROPE, 2, dtype=jnp.float32) / QK_ROPE)
    ang = pos[..., None] * inv_freq
    cos, sin = jnp.cos(ang), jnp.sin(ang)

    def slot(t):
        lead = ((0, 0),) * (t.ndim - 1)
        return jnp.pad(jnp.concatenate([t, t], axis=-1),
                       lead + ((QK_NOPE, HEAD_SLOT - QK_NOPE - QK_ROPE),))

    return jnp.swapaxes(cos, 1, 2), jnp.swapaxes(sin, 1, 2), slot(cos), slot(sin)


def kernel(x, positions, meta_tokens, ln_g, ln_b, mla_w_in, mla_q_norm_g, mla_w_uq, mla_kv_norm_g,
           mla_w_uk, mla_w_uv, mla_w_o, conv_w_in, conv_w, conv_w_out):
    bsz, seq, _ = x.shape
    assert seq % CONV_TILE == 0 and CONV_TILE % TOKEN_TILE == 0 and TOKEN_TILE % CHUNK == 0
    assert DEPTH % 2 == 0
    f32 = jnp.float32

    hf = x.astype(f32)
    hm = jnp.pad(meta_tokens.astype(f32), ((0, META_PAD - N_META), (0, 0)))[None]

    frame_tabs = _rope_tables((positions + N_META).astype(f32))
    meta_pos = jnp.pad(jnp.arange(N_META, dtype=f32), (0, META_PAD - N_META))[None]
    meta_tabs = _rope_tables(meta_pos)

    zero_carry = jnp.zeros((1, SUBLANES, CONV_WIDTH), f32)
    for j in range(DEPTH // 2):
        w = _prep_mla_weights(mla_w_in[j], mla_q_norm_g[j], mla_w_uq[j], mla_kv_norm_g[j],
                              mla_w_uk[j], mla_w_uv[j], mla_w_o[j])
        w.update({
            "g1": ln_g[2 * j].reshape(1, D_MODEL), "b1": ln_b[2 * j].reshape(1, D_MODEL),
            "g2": ln_g[2 * j + 1].reshape(1, D_MODEL), "b2": ln_b[2 * j + 1].reshape(1, D_MODEL),
            "win": _bf16(conv_w_in[j]), "wout": _bf16(conv_w_out[j]),
            "cw": jnp.pad(conv_w[j].astype(f32), ((0, SUBLANES - CONV_K), (0, 0))),
        })
        qt_m, k_m, vt_m, gate_m = _mla_proj(hm, *meta_tabs, w, META_PAD, causal_lanes=False)
        qt_f, k_f, vt_f, gate_f = _mla_proj(hf, *frame_tabs, w, TOKEN_TILE, causal_lanes=True)
        ot_m = _meta_attn(qt_m, k_m, vt_m)
        ot_f = _frame_attn(qt_f, k_f, vt_f, k_m, vt_m, TOKEN_TILE)
        hm, tail = _out_conv(ot_m, gate_m, hm, zero_carry, w, META_PAD, N_META - SUBLANES)
        hf, _ = _out_conv(ot_f, gate_f, hf, tail, w, CONV_TILE, 0)
    return hf
```

```python
import functools

import jax
import jax.numpy as jnp
from jax import lax
from jax.experimental import pallas as pl
from jax.experimental.pallas import tpu as pltpu

D_MODEL = 1024
DEPTH = 4
CHUNK = 64
CHUNK_SHIFT = CHUNK.bit_length() - 1
N_META = 16
MLA_HEADS = 16
QK_NOPE = 64
QK_ROPE = 32
HALF_ROPE = QK_ROPE // 2
V_HEAD = 64
Q_LORA = 384
KV_LORA = 256
MLA_WIDTH = MLA_HEADS * V_HEAD
ROPE_BASE = 10000.0
CONV_WIDTH = D_MODEL
CONV_K = 3
DN_ALPHA = (2 * DEPTH) ** 0.25
LN_EPS = 1e-5
RMS_EPS = 1e-6
NEG_INF = -1e30
LOG2_E = 1.4426950408889634

HEAD_SLOT = 128
QK_WIDTH = MLA_HEADS * HEAD_SLOT
MASK_LANE = QK_NOPE + QK_ROPE
BF16_ROWS = 16
V_SLOT = V_HEAD + BF16_ROWS
V_WIDTH = MLA_HEADS * V_SLOT
META_PAD = 128
TOKEN_TILE = 256
CONV_TILE = 512
SUBLANES = 8
VMEM_LIMIT = 56 * 1024 * 1024

_OFF_CKV = Q_LORA
_OFF_Z = Q_LORA + KV_LORA
_OFF_RA = _OFF_Z + MLA_WIDTH
_OFF_RB = _OFF_RA + HEAD_SLOT
_W1_COLS = _OFF_RB + HEAD_SLOT

_NT = (((1,), (1,)), ((), ()))


def _bf16(x):
    return x.astype(jnp.bfloat16)


def _dot(a, b):
    return jnp.dot(a, b, preferred_element_type=jnp.float32)


def _dot_nt(a, b):
    return lax.dot_general(a, b, _NT, preferred_element_type=jnp.float32)


def _rms(x, g):
    y = x * lax.rsqrt(jnp.mean(jnp.square(x), axis=-1, keepdims=True) + RMS_EPS)
    return y * g


def _layer_norm(x, g, b):
    mu = jnp.mean(x, axis=-1, keepdims=True)
    xc = x - mu
    var = jnp.mean(jnp.square(xc), axis=-1, keepdims=True)
    return xc * lax.rsqrt(var + LN_EPS) * g + b


def _silu(z):
    return z * (1.0 / (1.0 + jnp.exp(-z)))


def _mla_proj_kernel(h_ref, cos_t_ref, sin_t_ref,
                     w1_ref, gq_ref, gkv_ref, wuq_ref, wuk_ref, wuv_ref,
                     qt_ref, k_ref, vt_ref, gate_ref, *, causal_lanes):
    hb = _bf16(h_ref[0])
    p = _dot(hb, w1_ref[...])
    cqn = _bf16(_rms(p[:, :Q_LORA], gq_ref[...]))
    ckvn = _bf16(_rms(p[:, _OFF_CKV:_OFF_Z], gkv_ref[...]))
    gate_ref[0] = _bf16(_silu(p[:, _OFF_Z:_OFF_RA]))

    cos_t = cos_t_ref[0]
    sin_t = sin_t_ref[0]

    def slot_table(t):
        tm = t.shape[1]
        rows = [jnp.zeros((QK_NOPE, tm), t.dtype), t, t,
                jnp.zeros((HEAD_SLOT - QK_NOPE - QK_ROPE, tm), t.dtype)]
        return jnp.concatenate(rows, axis=0).T

    kr_slot = p[:, _OFF_RA:_OFF_RB] * slot_table(cos_t) + p[:, _OFF_RB:_W1_COLS] * slot_table(sin_t)
    if causal_lanes:
        row = lax.broadcasted_iota(jnp.int32, kr_slot.shape, 0)
        c = lax.broadcasted_iota(jnp.int32, kr_slot.shape, 1) - MASK_LANE
        later_chunk = (c >= 0) & (c < kr_slot.shape[0] // CHUNK - 1) & (row >> CHUNK_SHIFT > c)
        kr_slot = jnp.where(later_chunk, 1.0, kr_slot)
    k_all = _dot(ckvn, wuk_ref[...])
    for hd in range(MLA_HEADS):
        k_ref[0, hd] = _bf16(k_all[:, hd * HEAD_SLOT:(hd + 1) * HEAD_SLOT] + kr_slot)

    vt = _dot_nt(wuv_ref[...], ckvn)
    ones_row = lax.broadcasted_iota(jnp.int32, (BF16_ROWS, vt.shape[1]), 0) == 0
    for hd in range(MLA_HEADS):
        vt_ref[0, 0, hd * V_SLOT:hd * V_SLOT + V_HEAD, :] = _bf16(vt[_head_rows(hd, V_HEAD)])
        vt_ref[0, 0, hd * V_SLOT + V_HEAD:(hd + 1) * V_SLOT, :] = ones_row.astype(jnp.bfloat16)

    scale = (QK_NOPE + QK_ROPE) ** -0.5 * LOG2_E
    qt = _dot_nt(wuq_ref[...], cqn) * scale
    for hd in range(MLA_HEADS):
        base = hd * HEAD_SLOT
        x1 = qt[base + QK_NOPE:base + QK_NOPE + HALF_ROPE]
        x2 = qt[base + QK_NOPE + HALF_ROPE:base + QK_NOPE + QK_ROPE]
        qt_ref[0, base:base + QK_NOPE, :] = _bf16(qt[base:base + QK_NOPE])
        qt_ref[0, base + QK_NOPE:base + QK_NOPE + HALF_ROPE, :] = _bf16(x1 * cos_t - x2 * sin_t)
        qt_ref[0, base + QK_NOPE + HALF_ROPE:base + QK_NOPE + QK_ROPE, :] = _bf16(x1 * sin_t + x2 * cos_t)
        qt_ref[0, base + QK_NOPE + QK_ROPE:base + HEAD_SLOT, :] = jnp.zeros(
            (HEAD_SLOT - QK_NOPE - QK_ROPE, qt.shape[1]), jnp.bfloat16)


def _mla_proj(h, cos_t, sin_t, w, tm, causal_lanes):
    bx, lx, _ = h.shape
    nt = lx // tm
    tile = lambda b, i: (b, i, 0)
    tile_t = lambda b, i: (b, 0, i)
    const2 = lambda b, i: (0, 0)
    return pl.pallas_call(
        functools.partial(_mla_proj_kernel, causal_lanes=causal_lanes),
        grid=(bx, nt),
        in_specs=[
            pl.BlockSpec((1, tm, D_MODEL), tile),
            pl.BlockSpec((1, HALF_ROPE, tm), tile_t),
            pl.BlockSpec((1, HALF_ROPE, tm), tile_t),
            pl.BlockSpec((D_MODEL, _W1_COLS), const2),
            pl.BlockSpec((1, Q_LORA), const2),
            pl.BlockSpec((1, KV_LORA), const2),
            pl.BlockSpec((QK_WIDTH, Q_LORA), const2),
            pl.BlockSpec((KV_LORA, QK_WIDTH), const2),
            pl.BlockSpec((MLA_WIDTH, KV_LORA), const2),
        ],
        out_specs=[
            pl.BlockSpec((1, QK_WIDTH, tm), tile_t),
            pl.BlockSpec((1, MLA_HEADS, tm, HEAD_SLOT), lambda b, i: (b, 0, i, 0)),
            pl.BlockSpec((1, 1, V_WIDTH, tm), lambda b, i: (b, i, 0, 0)),
            pl.BlockSpec((1, tm, MLA_WIDTH), tile),
        ],
        out_shape=[
            jax.ShapeDtypeStruct((bx, QK_WIDTH, lx), jnp.bfloat16),
            jax.ShapeDtypeStruct((bx, MLA_HEADS, lx, HEAD_SLOT), jnp.bfloat16),
            jax.ShapeDtypeStruct((bx, nt, V_WIDTH, tm), jnp.bfloat16),
            jax.ShapeDtypeStruct((bx, lx, MLA_WIDTH), jnp.bfloat16),
        ],
        compiler_params=pltpu.CompilerParams(
            dimension_semantics=("parallel", "parallel"), vmem_limit_bytes=VMEM_LIMIT),
        name="mla_proj",
    )(h, cos_t, sin_t, w["w1"], w["gq"], w["gkv"], w["wuq_t"], w["wuk"], w["wuv_t"])


def _softmax_start(s):
    m = jnp.max(s, axis=0, keepdims=True)
    return m, jnp.exp2(s - m)


def _normalize(acc):
    return acc[:V_HEAD] / acc[V_HEAD:V_HEAD + 1]


def _meta_scores(q, km):
    s = _dot(km, q)
    key = lax.broadcasted_iota(jnp.int32, s.shape, 0)
    return jnp.where(key < N_META, s, NEG_INF)


def _head_rows(hd, width):
    return slice(hd * width, (hd + 1) * width)


def _meta_attn_kernel(qt_ref, km_ref, vm_ref, ot_ref):
    for hd in range(MLA_HEADS):
        q = qt_ref[0, _head_rows(hd, HEAD_SLOT), :]
        _, p = _softmax_start(_meta_scores(q, km_ref[0, hd]))
        acc = _dot(vm_ref[0, 0, _head_rows(hd, V_SLOT), :], _bf16(p))
        ot_ref[0, _head_rows(hd, V_HEAD), :] = _bf16(_normalize(acc))


def _frame_attn_kernel(qt_ref, qn_ref, kb_ref, vb_ref, km_ref, vm_ref, ot_ref,
                       kbuf_ref, vbuf_ref, s_ref, s0_ref, sm0_ref, m_ref, acc_ref):
    tk = vb_ref.shape[3]
    tq = qt_ref.shape[2]
    qi = pl.program_id(1)

    kbuf_ref[:, pl.ds(pl.multiple_of(qi * tk, tk), tk), :] = kb_ref[0]
    vbuf_ref[qi] = vb_ref[0, 0]

    row = lax.broadcasted_iota(jnp.int32, (BF16_ROWS, tq), 0)
    qry = lax.broadcasted_iota(jnp.int32, (BF16_ROWS, tq), 1)
    hide = (qry >> CHUNK_SHIFT == row) & (row < tk // CHUNK - 1)

    def mask_rows(is_diagonal):
        return _bf16(jnp.where(hide & is_diagonal, NEG_INF, 0.0))

    def scores(q_ref, hd, j, mask):
        q = q_ref[0, _head_rows(hd, HEAD_SLOT), :]
        if mask is not None:
            q = jnp.concatenate([q[:MASK_LANE], mask, q[MASK_LANE + BF16_ROWS:]], axis=0)
        row0 = pl.multiple_of(j * tk, tk)
        return _dot(kbuf_ref[hd, pl.ds(row0, tk), :], q)

    def update(hd, s, j):
        m = m_ref[hd]
        m_new = jnp.maximum(m, jnp.max(s, axis=0, keepdims=True))
        alpha = jnp.exp2(m - m_new)
        p = jnp.exp2(s - m_new)
        m_ref[hd] = m_new
        acc_ref[hd] = alpha * acc_ref[hd] + _dot(vbuf_ref[j, _head_rows(hd, V_SLOT), :], _bf16(p))

    def meta_scores():
        return [_dot(km_ref[0, hd], qt_ref[0, _head_rows(hd, HEAD_SLOT), :])
                for hd in range(MLA_HEADS)]

    def meta_values(hd, p):
        no_weight = jnp.zeros((META_PAD - N_META, tq), jnp.bfloat16)
        return _dot(vm_ref[0, 0, _head_rows(hd, V_SLOT), :],
                    jnp.concatenate([_bf16(p), no_weight], axis=0))

    @pl.when(qi == 0)
    def _():
        sm = meta_scores()
        diag_mask = mask_rows(True)
        for hd in range(MLA_HEADS):
            s_ref[0, hd] = scores(qt_ref, hd, 0, diag_mask)
            m, p = _softmax_start(sm[hd])
            m_ref[hd] = m
            acc_ref[hd] = meta_values(hd, p)

    @pl.when(qi > 0)
    def _():
        first_mask = mask_rows(qi == 1)
        for hd in range(MLA_HEADS):
            s_ref[1, hd] = scores(qt_ref, hd, 1, first_mask)
            sm = sm0_ref[hd]
            s = s0_ref[hd]
            m = jnp.maximum(jnp.max(sm, axis=0, keepdims=True), jnp.max(s, axis=0, keepdims=True))
            m_ref[hd] = m
            acc_ref[hd] = meta_values(hd, jnp.exp2(sm - m)) + _dot(
                vbuf_ref[0, _head_rows(hd, V_SLOT), :], _bf16(jnp.exp2(s - m)))

    def pair_step(jj, carry):
        j = 2 * jj + 1
        for hd in range(MLA_HEADS):
            s_ref[0, hd] = scores(qt_ref, hd, j + 1, None)
            update(hd, s_ref[1, hd], j)
        last_mask = mask_rows(j + 2 == qi)
        for hd in range(MLA_HEADS):
            s_ref[1, hd] = scores(qt_ref, hd, j + 2, last_mask)
            update(hd, s_ref[0, hd], j + 1)
        return carry

    lax.fori_loop(0, (qi - 1) // 2, pair_step, 0)

    @pl.when((qi >= 2) & (qi % 2 == 0))
    def _():
        diag_mask = mask_rows(True)
        for hd in range(MLA_HEADS):
            s_ref[0, hd] = scores(qt_ref, hd, qi, diag_mask)
            update(hd, s_ref[1, hd], qi - 1)

    for hd in range(MLA_HEADS):
        keys = jnp.concatenate([kbuf_ref[hd, 0:tk, :], km_ref[0, hd]], axis=0)
        s_next = _dot(keys, qn_ref[0, _head_rows(hd, HEAD_SLOT), :])
        s0_ref[hd] = s_next[:tk]
        sm0_ref[hd] = s_next[tk:]
        update(hd, s_ref[qi % 2, hd], qi)
    for hd in range(MLA_HEADS):
        ot_ref[0, _head_rows(hd, V_HEAD), :] = _bf16(_normalize(acc_ref[hd]))


def _meta_attn(qt, k, vt):
    return pl.pallas_call(
        _meta_attn_kernel,
        grid=(1,),
        in_specs=[
            pl.BlockSpec((1, QK_WIDTH, META_PAD), lambda i: (0, 0, 0)),
            pl.BlockSpec((1, MLA_HEADS, META_PAD, HEAD_SLOT), lambda i: (0, 0, 0, 0)),
            pl.BlockSpec((1, 1, V_WIDTH, META_PAD), lambda i: (0, 0, 0, 0)),
        ],
        out_specs=pl.BlockSpec((1, MLA_WIDTH, META_PAD), lambda i: (0, 0, 0)),
        out_shape=jax.ShapeDtypeStruct((1, MLA_WIDTH, META_PAD), jnp.bfloat16),
        name="meta_attn",
    )(qt, k, vt)


def _frame_attn(qt, k, vt, k_meta, vt_meta, tq):
    bx, _, lx = qt.shape
    nk, tk = vt.shape[1], vt.shape[3]
    assert tq == tk
    nq = lx // tq
    return pl.pallas_call(
        _frame_attn_kernel,
        grid=(bx, nq),
        in_specs=[
            pl.BlockSpec((1, QK_WIDTH, tq), lambda b, i: (b, 0, i)),
            pl.BlockSpec((1, QK_WIDTH, tq), lambda b, i: (b, 0, jnp.minimum(i + 1, nq - 1))),
            pl.BlockSpec((1, MLA_HEADS, tk, HEAD_SLOT), lambda b, i: (b, 0, i, 0)),
            pl.BlockSpec((1, 1, V_WIDTH, tk), lambda b, i: (b, i, 0, 0)),
            pl.BlockSpec((1, MLA_HEADS, N_META, HEAD_SLOT), lambda b, i: (0, 0, 0, 0)),
            pl.BlockSpec((1, 1, V_WIDTH, META_PAD), lambda b, i: (0, 0, 0, 0)),
        ],
        out_specs=pl.BlockSpec((1, MLA_WIDTH, tq), lambda b, i: (b, 0, i)),
        out_shape=jax.ShapeDtypeStruct((bx, MLA_WIDTH, lx), jnp.bfloat16),
        scratch_shapes=[
            pltpu.VMEM((MLA_HEADS, lx, HEAD_SLOT), jnp.bfloat16),
            pltpu.VMEM((nk, V_WIDTH, tk), jnp.bfloat16),
            pltpu.VMEM((2, MLA_HEADS, tk, tq), jnp.float32),
            pltpu.VMEM((MLA_HEADS, tk, tq), jnp.float32),
            pltpu.VMEM((MLA_HEADS, N_META, tq), jnp.float32),
            pltpu.VMEM((MLA_HEADS, 1, tq), jnp.float32),
            pltpu.VMEM((MLA_HEADS, V_SLOT, tq), jnp.float32),
        ],
        compiler_params=pltpu.CompilerParams(
            dimension_semantics=("parallel", "arbitrary"), vmem_limit_bytes=VMEM_LIMIT),
        name="frame_attn",
    )(qt, qt, k, vt, k_meta, vt_meta)


def _out_conv_kernel(ot_ref, gate_ref, h_ref, carry_ref, wo_ref, g1_ref, b1_ref,
                     win_ref, cw_ref, wout_ref, g2_ref, b2_ref,
                     out_ref, tail_ref, cu_ref, *, tail_start, chunk):
    tm = h_ref.shape[1]
    w = CONV_WIDTH

    @pl.when(pl.program_id(1) == 0)
    def _():
        cu_ref[0:SUBLANES, :] = carry_ref[0]

    chunks = [slice(c0, c0 + chunk) for c0 in range(0, tm, chunk)]

    attn = []
    for rows in chunks:
        o = ot_ref[0, :, rows].astype(jnp.float32).T
        y = _bf16(o * gate_ref[0, rows, :].astype(jnp.float32))
        attn.append(_dot(y, wo_ref[...]))

    hs, ys = [], []
    for rows, a in zip(chunks, attn):
        h = _layer_norm(DN_ALPHA * h_ref[0, rows, :] + a, g1_ref[...], b1_ref[...])
        hb = _bf16(h)
        cu = _dot(hb, win_ref[:, w:2 * w]) * _dot(hb, win_ref[:, 2 * w:3 * w])
        c0 = rows.start
        cu_ref[SUBLANES + c0:SUBLANES + c0 + chunk, :] = cu
        conv = cu_ref[SUBLANES - 2 + c0:SUBLANES - 2 + c0 + chunk, :] * cw_ref[0:1, :]
        conv = conv + cu_ref[SUBLANES - 1 + c0:SUBLANES - 1 + c0 + chunk, :] * cw_ref[1:2, :]
        conv = conv + cu * cw_ref[2:3, :]
        hs.append(h)
        ys.append(_dot(hb, win_ref[:, 0:w]) * conv * _silu(_dot(hb, win_ref[:, 3 * w:4 * w])))

    for rows, h, y in zip(chunks, hs, ys):
        r = DN_ALPHA * h + _dot(_bf16(y), wout_ref[...])
        out_ref[0, rows, :] = _layer_norm(r, g2_ref[...], b2_ref[...])
    tail_ref[0] = cu_ref[SUBLANES + tail_start:2 * SUBLANES + tail_start, :]
    cu_ref[0:SUBLANES, :] = cu_ref[tm:tm + SUBLANES, :]


def _out_conv(ot, gate, h, carry, w, tm, tail_start):
    bx, lx, _ = h.shape
    chunk = min(tm, TOKEN_TILE)
    tile = lambda b_, i: (b_, i, 0)
    once = pl.Buffered(1)
    const = lambda shape: pl.BlockSpec(shape, lambda b_, i: (0,) * len(shape), pipeline_mode=once)
    return pl.pallas_call(
        functools.partial(_out_conv_kernel, tail_start=tail_start, chunk=chunk),
        grid=(bx, lx // tm),
        in_specs=[
            pl.BlockSpec((1, MLA_WIDTH, tm), lambda b_, i: (b_, 0, i)),
            pl.BlockSpec((1, tm, MLA_WIDTH), tile),
            pl.BlockSpec((1, tm, D_MODEL), tile),
            const((1, SUBLANES, CONV_WIDTH)),
            const((MLA_WIDTH, D_MODEL)),
            const((1, D_MODEL)),
            const((1, D_MODEL)),
            const((D_MODEL, 4 * CONV_WIDTH)),
            const((SUBLANES, CONV_WIDTH)),
            const((CONV_WIDTH, D_MODEL)),
            const((1, D_MODEL)),
            const((1, D_MODEL)),
        ],
        out_specs=[
            pl.BlockSpec((1, tm, D_MODEL), tile),
            pl.BlockSpec((1, SUBLANES, CONV_WIDTH), lambda b_, i: (b_, 0, 0)),
        ],
        out_shape=[
            jax.ShapeDtypeStruct(h.shape, jnp.float32),
            jax.ShapeDtypeStruct((bx, SUBLANES, CONV_WIDTH), jnp.float32),
        ],
        scratch_shapes=[pltpu.VMEM((tm + SUBLANES, CONV_WIDTH), jnp.float32)],
        compiler_params=pltpu.CompilerParams(
            dimension_semantics=("arbitrary", "arbitrary"), vmem_limit_bytes=VMEM_LIMIT),
        name="out_conv",
    )(ot, gate, h, carry, w["wo"], w["g1"], w["b1"], w["win"], w["cw"], w["wout"], w["g2"], w["b2"])


def _prep_mla_weights(w_in, gq, w_uq, gkv, w_uk, w_uv, w_o):
    o_kr = Q_LORA + KV_LORA
    o_z = o_kr + QK_ROPE
    kr1 = w_in[:, o_kr:o_kr + HALF_ROPE]
    kr2 = w_in[:, o_kr + HALF_ROPE:o_z]
    zl = jnp.zeros((D_MODEL, QK_NOPE), w_in.dtype)
    zr = jnp.zeros((D_MODEL, HEAD_SLOT - QK_NOPE - QK_ROPE), w_in.dtype)
    w1 = jnp.concatenate([w_in[:, :o_kr], w_in[:, o_z:],
                          zl, kr1, kr2, zr, zl, -kr2, kr1, zr], axis=1)
    wuq = w_uq.reshape(Q_LORA, MLA_HEADS, QK_NOPE + QK_ROPE)
    wuq = jnp.pad(wuq, ((0, 0), (0, 0), (0, HEAD_SLOT - QK_NOPE - QK_ROPE)))
    wuk = jnp.pad(w_uk.reshape(KV_LORA, MLA_HEADS, QK_NOPE), ((0, 0), (0, 0), (0, HEAD_SLOT - QK_NOPE)))
    return {
        "w1": _bf16(w1),
        "gq": gq.reshape(1, Q_LORA),
        "gkv": gkv.reshape(1, KV_LORA),
        "wuq_t": _bf16(wuq.reshape(Q_LORA, QK_WIDTH).T),
        "wuk": _bf16(wuk.reshape(KV_LORA, QK_WIDTH)),
        "wuv_t": _bf16(w_uv.T),
        "wo": _bf16(w_o),
    }


def _rope_tables(pos):
    inv_freq = ROPE_BASE ** (-jnp.arange(0, QK_ROPE, 2, dtype=jnp.float32) / QK_ROPE)
    ang = pos[..., None] * inv_freq
    return jnp.swapaxes(jnp.cos(ang), 1, 2), jnp.swapaxes(jnp.sin(ang), 1, 2)


def kernel(x, positions, meta_tokens, ln_g, ln_b, mla_w_in, mla_q_norm_g, mla_w_uq, mla_kv_norm_g,
           mla_w_uk, mla_w_uv, mla_w_o, conv_w_in, conv_w, conv_w_out):
    bsz, seq, _ = x.shape
    assert seq % CONV_TILE == 0 and CONV_TILE % TOKEN_TILE == 0 and TOKEN_TILE % CHUNK == 0
    assert DEPTH % 2 == 0
    f32 = jnp.float32

    hf = x.astype(f32)
    hm = jnp.pad(meta_tokens.astype(f32), ((0, META_PAD - N_META), (0, 0)))[None]

    frame_tabs = _rope_tables((positions + N_META).astype(f32))
    meta_pos = jnp.pad(jnp.arange(N_META, dtype=f32), (0, META_PAD - N_META))[None]
    meta_tabs = _rope_tables(meta_pos)

    zero_carry = jnp.zeros((1, SUBLANES, CONV_WIDTH), f32)
    for j in range(DEPTH // 2):
        w = _prep_mla_weights(mla_w_in[j], mla_q_norm_g[j], mla_w_uq[j], mla_kv_norm_g[j],
                              mla_w_uk[j], mla_w_uv[j], mla_w_o[j])
        w.update({
            "g1": ln_g[2 * j].reshape(1, D_MODEL), "b1": ln_b[2 * j].reshape(1, D_MODEL),
            "g2": ln_g[2 * j + 1].reshape(1, D_MODEL), "b2": ln_b[2 * j + 1].reshape(1, D_MODEL),
            "win": _bf16(conv_w_in[j]), "wout": _bf16(conv_w_out[j]),
            "cw": jnp.pad(conv_w[j].astype(f32), ((0, SUBLANES - CONV_K), (0, 0))),
        })
        qt_m, k_m, vt_m, gate_m = _mla_proj(hm, *meta_tabs, w, META_PAD, causal_lanes=False)
        qt_f, k_f, vt_f, gate_f = _mla_proj(hf, *frame_tabs, w, TOKEN_TILE, causal_lanes=True)
        ot_m = _meta_attn(qt_m, k_m, vt_m)
        ot_f = _frame_attn(qt_f, k_f, vt_f, k_m, vt_m, TOKEN_TILE)
        hm, tail = _out_conv(ot_m, gate_m, hm, zero_carry, w, META_PAD, N_META - SUBLANES)
        hf, _ = _out_conv(ot_f, gate_f, hf, tail, w, CONV_TILE, 0)
    return hf
```

```python
import functools

import jax
import jax.numpy as jnp
from jax import lax
from jax.experimental import pallas as pl
from jax.experimental.pallas import tpu as pltpu

D_MODEL = 1024
DEPTH = 4
CHUNK = 64
CHUNK_SHIFT = CHUNK.bit_length() - 1
N_META = 16
MLA_HEADS = 16
QK_NOPE = 64
QK_ROPE = 32
HALF_ROPE = QK_ROPE // 2
V_HEAD = 64
Q_LORA = 384
KV_LORA = 256
MLA_WIDTH = MLA_HEADS * V_HEAD
ROPE_BASE = 10000.0
CONV_WIDTH = D_MODEL
CONV_K = 3
DN_ALPHA = (2 * DEPTH) ** 0.25
LN_EPS = 1e-5
RMS_EPS = 1e-6
NEG_INF = -1e30
LOG2_E = 1.4426950408889634

QK_HEAD = QK_NOPE + QK_ROPE
QK_WIDTH = MLA_HEADS * QK_HEAD
HEAD_SLOT = 128
ODD_SHIFT = HEAD_SLOT - QK_NOPE
BF16_ROWS = 16
V_SLOT = V_HEAD + BF16_ROWS
V_WIDTH = MLA_HEADS * V_SLOT
META_PAD = 128
TOKEN_TILE = 256
CONV_TILE = 512
SUBLANES = 8
VMEM_LIMIT = 56 * 1024 * 1024

_OFF_CKV = Q_LORA
_OFF_Z = Q_LORA + KV_LORA
_OFF_ROPE = _OFF_Z + MLA_WIDTH
_W1_COLS = _OFF_ROPE + HEAD_SLOT

_NT = (((1,), (1,)), ((), ()))


def _bf16(x):
    return x.astype(jnp.bfloat16)


def _dot(a, b):
    return jnp.dot(a, b, preferred_element_type=jnp.float32)


def _dot_nt(a, b):
    return lax.dot_general(a, b, _NT, preferred_element_type=jnp.float32)


def _rms(x, g):
    y = x * lax.rsqrt(jnp.mean(jnp.square(x), axis=-1, keepdims=True) + RMS_EPS)
    return y * g


def _layer_norm(x, g, b):
    mu = jnp.mean(x, axis=-1, keepdims=True)
    xc = x - mu
    var = jnp.mean(jnp.square(xc), axis=-1, keepdims=True)
    return xc * lax.rsqrt(var + LN_EPS) * g + b


def _silu(z):
    return z * (1.0 / (1.0 + jnp.exp(-z)))


def _mla_proj_kernel(h_ref, cos_t_ref, sin_t_ref,
                     w1_ref, gq_ref, gkv_ref, wuq_ref, wuk_ref, wuv_ref,
                     qt_ref, k_ref, vt_ref, gate_ref, *, causal_lanes):
    hb = _bf16(h_ref[0])
    p = _dot(hb, w1_ref[...])
    cqn = _bf16(_rms(p[:, :Q_LORA], gq_ref[...]))
    ckvn = _bf16(_rms(p[:, _OFF_CKV:_OFF_Z], gkv_ref[...]))
    gate_ref[0] = _bf16(_silu(p[:, _OFF_Z:_OFF_ROPE]))

    cos_t = cos_t_ref[0]
    sin_t = sin_t_ref[0]

    def slot_table(t):
        tm = t.shape[1]
        rows = [jnp.zeros((QK_NOPE, tm), t.dtype), t, t,
                jnp.zeros((HEAD_SLOT - QK_HEAD, tm), t.dtype)]
        return jnp.concatenate(rows, axis=0).T

    ab = p[:, _OFF_ROPE:_W1_COLS]
    kr_even = ab * slot_table(cos_t) + pltpu.roll(ab, HEAD_SLOT - QK_ROPE, 1) * slot_table(sin_t)
    if causal_lanes:
        row = lax.broadcasted_iota(jnp.int32, kr_even.shape, 0)
        c = lax.broadcasted_iota(jnp.int32, kr_even.shape, 1) - QK_HEAD
        later_chunk = (c >= 0) & (c < kr_even.shape[0] // CHUNK - 1) & (row >> CHUNK_SHIFT > c)
        kr_even = jnp.where(later_chunk, 1.0, kr_even)
    kr_odd = pltpu.roll(kr_even, ODD_SHIFT, 1)
    kn = _dot(ckvn, wuk_ref[...])
    low_half = lax.broadcasted_iota(jnp.int32, kr_even.shape, 1) < QK_NOPE
    for hd in range(0, MLA_HEADS, 2):
        pair = kn[:, hd * QK_NOPE:(hd + 2) * QK_NOPE]
        k_ref[0, hd] = _bf16(jnp.where(low_half, pair, kr_even))
        k_ref[0, hd + 1] = _bf16(jnp.where(low_half, kr_odd, pair))

    vt = _dot_nt(wuv_ref[...], ckvn)
    ones_row = lax.broadcasted_iota(jnp.int32, (BF16_ROWS, vt.shape[1]), 0) == 0
    for hd in range(MLA_HEADS):
        vt_ref[0, 0, hd * V_SLOT:hd * V_SLOT + V_HEAD, :] = _bf16(vt[_head_rows(hd, V_HEAD)])
        vt_ref[0, 0, hd * V_SLOT + V_HEAD:(hd + 1) * V_SLOT, :] = ones_row.astype(jnp.bfloat16)

    scale = QK_HEAD ** -0.5 * LOG2_E
    qt = _dot_nt(wuq_ref[...], cqn) * scale
    for hd in range(MLA_HEADS):
        base = hd * QK_HEAD
        nope = _bf16(qt[base:base + QK_NOPE])
        x1 = qt[base + QK_NOPE:base + QK_NOPE + HALF_ROPE]
        x2 = qt[base + QK_NOPE + HALF_ROPE:base + QK_HEAD]
        rope = _bf16(jnp.concatenate([x1 * cos_t - x2 * sin_t, x1 * sin_t + x2 * cos_t], axis=0))
        if hd % 2 == 0:
            qt_ref[0, base:base + QK_NOPE, :] = nope
            qt_ref[0, base + QK_NOPE:base + QK_HEAD, :] = rope
        else:
            qt_ref[0, base:base + QK_ROPE, :] = rope
            qt_ref[0, base + QK_ROPE:base + QK_HEAD, :] = nope


def _mla_proj(h, cos_t, sin_t, w, tm, causal_lanes):
    bx, lx, _ = h.shape
    nt = lx // tm
    tile = lambda b, i: (b, i, 0)
    tile_t = lambda b, i: (b, 0, i)
    const2 = lambda b, i: (0, 0)
    return pl.pallas_call(
        functools.partial(_mla_proj_kernel, causal_lanes=causal_lanes),
        grid=(bx, nt),
        in_specs=[
            pl.BlockSpec((1, tm, D_MODEL), tile),
            pl.BlockSpec((1, HALF_ROPE, tm), tile_t),
            pl.BlockSpec((1, HALF_ROPE, tm), tile_t),
            pl.BlockSpec((D_MODEL, _W1_COLS), const2),
            pl.BlockSpec((1, Q_LORA), const2),
            pl.BlockSpec((1, KV_LORA), const2),
            pl.BlockSpec((QK_WIDTH, Q_LORA), const2),
            pl.BlockSpec((KV_LORA, MLA_HEADS * QK_NOPE), const2),
            pl.BlockSpec((MLA_WIDTH, KV_LORA), const2),
        ],
        out_specs=[
            pl.BlockSpec((1, QK_WIDTH, tm), tile_t),
            pl.BlockSpec((1, MLA_HEADS, tm, HEAD_SLOT), lambda b, i: (b, 0, i, 0)),
            pl.BlockSpec((1, 1, V_WIDTH, tm), lambda b, i: (b, i, 0, 0)),
            pl.BlockSpec((1, tm, MLA_WIDTH), tile),
        ],
        out_shape=[
            jax.ShapeDtypeStruct((bx, QK_WIDTH, lx), jnp.bfloat16),
            jax.ShapeDtypeStruct((bx, MLA_HEADS, lx, HEAD_SLOT), jnp.bfloat16),
            jax.ShapeDtypeStruct((bx, nt, V_WIDTH, tm), jnp.bfloat16),
            jax.ShapeDtypeStruct((bx, lx, MLA_WIDTH), jnp.bfloat16),
        ],
        compiler_params=pltpu.CompilerParams(
            dimension_semantics=("parallel", "parallel"), vmem_limit_bytes=VMEM_LIMIT),
        name="mla_proj",
    )(h, cos_t, sin_t, w["w1"], w["gq"], w["gkv"], w["wuq_t"], w["wuk"], w["wuv_t"])


def _softmax_start(s):
    m = jnp.max(s, axis=0, keepdims=True)
    return m, jnp.exp2(s - m)


def _normalize(acc):
    return acc[:V_HEAD] / acc[V_HEAD:V_HEAD + 1]


def _meta_scores(q, km):
    s = _dot(km, q)
    key = lax.broadcasted_iota(jnp.int32, s.shape, 0)
    return jnp.where(key < N_META, s, NEG_INF)


def _head_rows(hd, width):
    return slice(hd * width, (hd + 1) * width)


def _q_operand(q_ref, hd, mask):
    q = q_ref[0, _head_rows(hd, QK_HEAD), :]
    zero = jnp.zeros((BF16_ROWS, q.shape[1]), q.dtype)
    spare = [zero if mask is None else mask, zero]
    parts = [q] + spare if hd % 2 == 0 else [q[:QK_ROPE]] + spare + [q[QK_ROPE:]]
    return jnp.concatenate(parts, axis=0)


def _meta_attn_kernel(qt_ref, km_ref, vm_ref, ot_ref):
    for hd in range(MLA_HEADS):
        q = _q_operand(qt_ref, hd, None)
        _, p = _softmax_start(_meta_scores(q, km_ref[0, hd]))
        acc = _dot(vm_ref[0, 0, _head_rows(hd, V_SLOT), :], _bf16(p))
        ot_ref[0, _head_rows(hd, V_HEAD), :] = _bf16(_normalize(acc))


def _frame_attn_kernel(qt_ref, qn_ref, kb_ref, vb_ref, km_ref, vm_ref, ot_ref,
                       kbuf_ref, vbuf_ref, s_ref, s0_ref, sm0_ref, m_ref, acc_ref):
    tk = vb_ref.shape[3]
    tq = qt_ref.shape[2]
    qi = pl.program_id(1)

    kbuf_ref[:, pl.ds(pl.multiple_of(qi * tk, tk), tk), :] = kb_ref[0]
    vbuf_ref[qi] = vb_ref[0, 0]

    row = lax.broadcasted_iota(jnp.int32, (BF16_ROWS, tq), 0)
    qry = lax.broadcasted_iota(jnp.int32, (BF16_ROWS, tq), 1)
    hide = (qry >> CHUNK_SHIFT == row) & (row < tk // CHUNK - 1)

    def mask_rows(is_diagonal):
        return _bf16(jnp.where(hide & is_diagonal, NEG_INF, 0.0))

    def scores(q_ref, hd, j, mask):
        row0 = pl.multiple_of(j * tk, tk)
        return _dot(kbuf_ref[hd, pl.ds(row0, tk), :], _q_operand(q_ref, hd, mask))

    def update(hd, s, j):
        m = m_ref[hd]
        m_new = jnp.maximum(m, jnp.max(s, axis=0, keepdims=True))
        alpha = jnp.exp2(m - m_new)
        p = jnp.exp2(s - m_new)
        m_ref[hd] = m_new
        acc_ref[hd] = alpha * acc_ref[hd] + _dot(vbuf_ref[j, _head_rows(hd, V_SLOT), :], _bf16(p))

    def meta_scores():
        return [_dot(km_ref[0, hd], _q_operand(qt_ref, hd, None))
                for hd in range(MLA_HEADS)]

    def meta_values(hd, p):
        no_weight = jnp.zeros((META_PAD - N_META, tq), jnp.bfloat16)
        return _dot(vm_ref[0, 0, _head_rows(hd, V_SLOT), :],
                    jnp.concatenate([_bf16(p), no_weight], axis=0))

    @pl.when(qi == 0)
    def _():
        sm = meta_scores()
        diag_mask = mask_rows(True)
        for hd in range(MLA_HEADS):
            s_ref[0, hd] = scores(qt_ref, hd, 0, diag_mask)
            m, p = _softmax_start(sm[hd])
            m_ref[hd] = m
            acc_ref[hd] = meta_values(hd, p)

    @pl.when(qi > 0)
    def _():
        first_mask = mask_rows(qi == 1)
        for hd in range(MLA_HEADS):
            s_ref[1, hd] = scores(qt_ref, hd, 1, first_mask)
            sm = sm0_ref[hd]
            s = s0_ref[hd]
            m = jnp.maximum(jnp.max(sm, axis=0, keepdims=True), jnp.max(s, axis=0, keepdims=True))
            m_ref[hd] = m
            acc_ref[hd] = meta_values(hd, jnp.exp2(sm - m)) + _dot(
                vbuf_ref[0, _head_rows(hd, V_SLOT), :], _bf16(jnp.exp2(s - m)))

    def pair_step(jj, carry):
        j = 2 * jj + 1
        for hd in range(MLA_HEADS):
            s_ref[0, hd] = scores(qt_ref, hd, j + 1, None)
            update(hd, s_ref[1, hd], j)
        last_mask = mask_rows(j + 2 == qi)
        for hd in range(MLA_HEADS):
            s_ref[1, hd] = scores(qt_ref, hd, j + 2, last_mask)
            update(hd, s_ref[0, hd], j + 1)
        return carry

    lax.fori_loop(0, (qi - 1) // 2, pair_step, 0)

    @pl.when((qi >= 2) & (qi % 2 == 0))
    def _():
        diag_mask = mask_rows(True)
        for hd in range(MLA_HEADS):
            s_ref[0, hd] = scores(qt_ref, hd, qi, diag_mask)
            update(hd, s_ref[1, hd], qi - 1)

    for hd in range(MLA_HEADS):
        keys = jnp.concatenate([kbuf_ref[hd, 0:tk, :], km_ref[0, hd]], axis=0)
        s_next = _dot(keys, _q_operand(qn_ref, hd, None))
        s0_ref[hd] = s_next[:tk]
        sm0_ref[hd] = s_next[tk:]
        update(hd, s_ref[qi % 2, hd], qi)
    for hd in range(MLA_HEADS):
        ot_ref[0, _head_rows(hd, V_HEAD), :] = _bf16(_normalize(acc_ref[hd]))


def _meta_attn(qt, k, vt):
    return pl.pallas_call(
        _meta_attn_kernel,
        grid=(1,),
        in_specs=[
            pl.BlockSpec((1, QK_WIDTH, META_PAD), lambda i: (0, 0, 0)),
            pl.BlockSpec((1, MLA_HEADS, META_PAD, HEAD_SLOT), lambda i: (0, 0, 0, 0)),
            pl.BlockSpec((1, 1, V_WIDTH, META_PAD), lambda i: (0, 0, 0, 0)),
        ],
        out_specs=pl.BlockSpec((1, MLA_WIDTH, META_PAD), lambda i: (0, 0, 0)),
        out_shape=jax.ShapeDtypeStruct((1, MLA_WIDTH, META_PAD), jnp.bfloat16),
        name="meta_attn",
    )(qt, k, vt)


def _frame_attn(qt, k, vt, k_meta, vt_meta, tq):
    bx, _, lx = qt.shape
    nk, tk = vt.shape[1], vt.shape[3]
    assert tq == tk
    nq = lx // tq
    return pl.pallas_call(
        _frame_attn_kernel,
        grid=(bx, nq),
        in_specs=[
            pl.BlockSpec((1, QK_WIDTH, tq), lambda b, i: (b, 0, i)),
            pl.BlockSpec((1, QK_WIDTH, tq), lambda b, i: (b, 0, jnp.minimum(i + 1, nq - 1))),
            pl.BlockSpec((1, MLA_HEADS, tk, HEAD_SLOT), lambda b, i: (b, 0, i, 0)),
            pl.BlockSpec((1, 1, V_WIDTH, tk), lambda b, i: (b, i, 0, 0)),
            pl.BlockSpec((1, MLA_HEADS, N_META, HEAD_SLOT), lambda b, i: (0, 0, 0, 0)),
            pl.BlockSpec((1, 1, V_WIDTH, META_PAD), lambda b, i: (0, 0, 0, 0)),
        ],
        out_specs=pl.BlockSpec((1, MLA_WIDTH, tq), lambda b, i: (b, 0, i)),
        out_shape=jax.ShapeDtypeStruct((bx, MLA_WIDTH, lx), jnp.bfloat16),
        scratch_shapes=[
            pltpu.VMEM((MLA_HEADS, lx, HEAD_SLOT), jnp.bfloat16),
            pltpu.VMEM((nk, V_WIDTH, tk), jnp.bfloat16),
            pltpu.VMEM((2, MLA_HEADS, tk, tq), jnp.float32),
            pltpu.VMEM((MLA_HEADS, tk, tq), jnp.float32),
            pltpu.VMEM((MLA_HEADS, N_META, tq), jnp.float32),
            pltpu.VMEM((MLA_HEADS, 1, tq), jnp.float32),
            pltpu.VMEM((MLA_HEADS, V_SLOT, tq), jnp.float32),
        ],
        compiler_params=pltpu.CompilerParams(
            dimension_semantics=("parallel", "arbitrary"), vmem_limit_bytes=VMEM_LIMIT),
        name="frame_attn",
    )(qt, qt, k, vt, k_meta, vt_meta)


def _out_conv_kernel(ot_ref, gate_ref, h_ref, carry_ref, wo_ref, g1_ref, b1_ref,
                     win_ref, cw_ref, wout_ref, g2_ref, b2_ref,
                     out_ref, tail_ref, cu_ref, *, tail_start, chunk):
    tm = h_ref.shape[1]
    w = CONV_WIDTH

    @pl.when(pl.program_id(1) == 0)
    def _():
        cu_ref[0:SUBLANES, :] = carry_ref[0]

    chunks = [slice(c0, c0 + chunk) for c0 in range(0, tm, chunk)]

    attn = []
    for rows in chunks:
        o = ot_ref[0, :, rows].astype(jnp.float32).T
        y = _bf16(o * gate_ref[0, rows, :].astype(jnp.float32))
        attn.append(_dot(y, wo_ref[...]))

    hs, ys = [], []
    for rows, a in zip(chunks, attn):
        h = _layer_norm(DN_ALPHA * h_ref[0, rows, :] + a, g1_ref[...], b1_ref[...])
        hb = _bf16(h)
        cu = _dot(hb, win_ref[:, w:2 * w]) * _dot(hb, win_ref[:, 2 * w:3 * w])
        c0 = rows.start
        cu_ref[SUBLANES + c0:SUBLANES + c0 + chunk, :] = cu
        conv = cu_ref[SUBLANES - 2 + c0:SUBLANES - 2 + c0 + chunk, :] * cw_ref[0:1, :]
        conv = conv + cu_ref[SUBLANES - 1 + c0:SUBLANES - 1 + c0 + chunk, :] * cw_ref[1:2, :]
        conv = conv + cu * cw_ref[2:3, :]
        hs.append(h)
        ys.append(_dot(hb, win_ref[:, 0:w]) * conv * _silu(_dot(hb, win_ref[:, 3 * w:4 * w])))

    for rows, h, y in zip(chunks, hs, ys):
        r = DN_ALPHA * h + _dot(_bf16(y), wout_ref[...])
        out_ref[0, rows, :] = _layer_norm(r, g2_ref[...], b2_ref[...])
    tail_ref[0] = cu_ref[SUBLANES + tail_start:2 * SUBLANES + tail_start, :]
    cu_ref[0:SUBLANES, :] = cu_ref[tm:tm + SUBLANES, :]


def _out_conv(ot, gate, h, carry, w, tm, tail_start):
    bx, lx, _ = h.shape
    chunk = min(tm, TOKEN_TILE)
    tile = lambda b_, i: (b_, i, 0)
    once = pl.Buffered(1)
    const = lambda shape: pl.BlockSpec(shape, lambda b_, i: (0,) * len(shape), pipeline_mode=once)
    return pl.pallas_call(
        functools.partial(_out_conv_kernel, tail_start=tail_start, chunk=chunk),
        grid=(bx, lx // tm),
        in_specs=[
            pl.BlockSpec((1, MLA_WIDTH, tm), lambda b_, i: (b_, 0, i)),
            pl.BlockSpec((1, tm, MLA_WIDTH), tile),
            pl.BlockSpec((1, tm, D_MODEL), tile),
            const((1, SUBLANES, CONV_WIDTH)),
            const((MLA_WIDTH, D_MODEL)),
            const((1, D_MODEL)),
            const((1, D_MODEL)),
            const((D_MODEL, 4 * CONV_WIDTH)),
            const((SUBLANES, CONV_WIDTH)),
            const((CONV_WIDTH, D_MODEL)),
            const((1, D_MODEL)),
            const((1, D_MODEL)),
        ],
        out_specs=[
            pl.BlockSpec((1, tm, D_MODEL), tile),
            pl.BlockSpec((1, SUBLANES, CONV_WIDTH), lambda b_, i: (b_, 0, 0)),
        ],
        out_shape=[
            jax.ShapeDtypeStruct(h.shape, jnp.float32),
            jax.ShapeDtypeStruct((bx, SUBLANES, CONV_WIDTH), jnp.float32),
        ],
        scratch_shapes=[pltpu.VMEM((tm + SUBLANES, CONV_WIDTH), jnp.float32)],
        compiler_params=pltpu.CompilerParams(
            dimension_semantics=("arbitrary", "arbitrary"), vmem_limit_bytes=VMEM_LIMIT),
        name="out_conv",
    )(ot, gate, h, carry, w["wo"], w["g1"], w["b1"], w["win"], w["cw"], w["wout"], w["g2"], w["b2"])


def _prep_mla_weights(w_in, gq, w_uq, gkv, w_uk, w_uv, w_o):
    o_kr = Q_LORA + KV_LORA
    o_z = o_kr + QK_ROPE
    kr1 = w_in[:, o_kr:o_kr + HALF_ROPE]
    kr2 = w_in[:, o_kr + HALF_ROPE:o_z]
    zl = jnp.zeros((D_MODEL, QK_NOPE), w_in.dtype)
    w1 = jnp.concatenate([w_in[:, :o_kr], w_in[:, o_z:], zl, kr1, kr2, -kr2, kr1], axis=1)
    return {
        "w1": _bf16(w1),
        "gq": gq.reshape(1, Q_LORA),
        "gkv": gkv.reshape(1, KV_LORA),
        "wuq_t": _bf16(w_uq.T),
        "wuk": _bf16(w_uk),
        "wuv_t": _bf16(w_uv.T),
        "wo": _bf16(w_o),
    }


def _rope_tables(pos):
    inv_freq = ROPE_BASE ** (-jnp.arange(0, QK_ROPE, 2, dtype=jnp.float32) / QK_ROPE)
    ang = pos[..., None] * inv_freq
    return jnp.swapaxes(jnp.cos(ang), 1, 2), jnp.swapaxes(jnp.sin(ang), 1, 2)


def kernel(x, positions, meta_tokens, ln_g, ln_b, mla_w_in, mla_q_norm_g, mla_w_uq, mla_kv_norm_g,
           mla_w_uk, mla_w_uv, mla_w_o, conv_w_in, conv_w, conv_w_out):
    bsz, seq, _ = x.shape
    assert seq % CONV_TILE == 0 and CONV_TILE % TOKEN_TILE == 0 and TOKEN_TILE % CHUNK == 0
    assert DEPTH % 2 == 0
    f32 = jnp.float32

    hf = x.astype(f32)
    hm = jnp.pad(meta_tokens.astype(f32), ((0, META_PAD - N_META), (0, 0)))[None]

    frame_tabs = _rope_tables((positions + N_META).astype(f32))
    meta_pos = jnp.pad(jnp.arange(N_META, dtype=f32), (0, META_PAD - N_META))[None]
    meta_tabs = _rope_tables(meta_pos)

    zero_carry = jnp.zeros((1, SUBLANES, CONV_WIDTH), f32)
    for j in range(DEPTH // 2):
        w = _prep_mla_weights(mla_w_in[j], mla_q_norm_g[j], mla_w_uq[j], mla_kv_norm_g[j],
                              mla_w_uk[j], mla_w_uv[j], mla_w_o[j])
        w.update({
            "g1": ln_g[2 * j].reshape(1, D_MODEL), "b1": ln_b[2 * j].reshape(1, D_MODEL),
            "g2": ln_g[2 * j + 1].reshape(1, D_MODEL), "b2": ln_b[2 * j + 1].reshape(1, D_MODEL),
            "win": _bf16(conv_w_in[j]), "wout": _bf16(conv_w_out[j]),
            "cw": jnp.pad(conv_w[j].astype(f32), ((0, SUBLANES - CONV_K), (0, 0))),
        })
        qt_m, k_m, vt_m, gate_m = _mla_proj(hm, *meta_tabs, w, META_PAD, causal_lanes=False)
        qt_f, k_f, vt_f, gate_f = _mla_proj(hf, *frame_tabs, w, TOKEN_TILE, causal_lanes=True)
        ot_m = _meta_attn(qt_m, k_m, vt_m)
        ot_f = _frame_attn(qt_f, k_f, vt_f, k_m, vt_m, TOKEN_TILE)
        hm, tail = _out_conv(ot_m, gate_m, hm, zero_carry, w, META_PAD, N_META - SUBLANES)
        hf, _ = _out_conv(ot_f, gate_f, hf, tail, w, CONV_TILE, 0)
    return hf
```

```python
import jax
import jax.numpy as jnp
from jax import lax
from jax.experimental import pallas as pl
from jax.experimental.pallas import tpu as pltpu

D_MODEL = 1024
DEPTH = 4
CHUNK = 64
CHUNK_SHIFT = CHUNK.bit_length() - 1
N_META = 16
MLA_HEADS = 16
QK_NOPE = 64
QK_ROPE = 32
HALF_ROPE = QK_ROPE // 2
V_HEAD = 64
Q_LORA = 384
KV_LORA = 256
MLA_WIDTH = MLA_HEADS * V_HEAD
ROPE_BASE = 10000.0
CONV_WIDTH = D_MODEL
CONV_K = 3
DN_ALPHA = (2 * DEPTH) ** 0.25
LN_EPS = 1e-5
RMS_EPS = 1e-6
NEG_INF = -1e30
LOG2_E = 1.4426950408889634

QK_HEAD = QK_NOPE + QK_ROPE
QK_WIDTH = MLA_HEADS * QK_HEAD
HEAD_SLOT = 128
ODD_SHIFT = HEAD_SLOT - QK_NOPE
BF16_ROWS = 16
V_SLOT = V_HEAD + BF16_ROWS
V_WIDTH = MLA_HEADS * V_SLOT
META_PAD = 128
TOKEN_TILE = 256
CONV_TILE = 512
SUBLANES = 8
VMEM_LIMIT = 56 * 1024 * 1024

_OFF_CKV = Q_LORA
_OFF_Z = Q_LORA + KV_LORA
_OFF_ROPE = _OFF_Z + MLA_WIDTH
_W1_COLS = _OFF_ROPE + HEAD_SLOT

_NT = (((1,), (1,)), ((), ()))


def _bf16(x):
    return x.astype(jnp.bfloat16)


def _dot(a, b):
    return jnp.dot(a, b, preferred_element_type=jnp.float32)


def _dot_nt(a, b):
    return lax.dot_general(a, b, _NT, preferred_element_type=jnp.float32)


def _rms(x, g):
    y = x * lax.rsqrt(jnp.mean(jnp.square(x), axis=-1, keepdims=True) + RMS_EPS)
    return y * g


def _layer_norm(x, g, b):
    mu = jnp.mean(x, axis=-1, keepdims=True)
    xc = x - mu
    var = jnp.mean(jnp.square(xc), axis=-1, keepdims=True)
    return xc * lax.rsqrt(var + LN_EPS) * g + b


def _silu(z):
    return z * (1.0 / (1.0 + jnp.exp(-z)))


def _head_rows(hd, width):
    return slice(hd * width, (hd + 1) * width)


def _first_step():
    return (pl.program_id(0) == 0) & (pl.program_id(1) == 0)


def _const_spec(shape, pipeline_mode=None):
    return pl.BlockSpec(shape, lambda b, i: (0,) * len(shape), pipeline_mode=pipeline_mode)


def _mla_proj_tile(h_ref, cos_t_ref, sin_t_ref, w1_ref, gq_ref, gkv_ref, wuq_ref, wuk_ref, wuv_ref,
                   qt_ref, k_ref, vt_ref, gate_ref, causal_lanes):
    hb = _bf16(h_ref[0])
    p = _dot(hb, w1_ref[...])
    cqn = _bf16(_rms(p[:, :Q_LORA], gq_ref[...]))
    ckvn = _bf16(_rms(p[:, _OFF_CKV:_OFF_Z], gkv_ref[...]))
    gate_ref[0] = _bf16(_silu(p[:, _OFF_Z:_OFF_ROPE]))

    cos_t = cos_t_ref[0]
    sin_t = sin_t_ref[0]

    def slot_table(t):
        tm = t.shape[1]
        rows = [jnp.zeros((QK_NOPE, tm), t.dtype), t, t,
                jnp.zeros((HEAD_SLOT - QK_HEAD, tm), t.dtype)]
        return jnp.concatenate(rows, axis=0).T

    ab = p[:, _OFF_ROPE:_W1_COLS]
    kr_even = ab * slot_table(cos_t) + pltpu.roll(ab, HEAD_SLOT - QK_ROPE, 1) * slot_table(sin_t)
    if causal_lanes:
        row = lax.broadcasted_iota(jnp.int32, kr_even.shape, 0)
        c = lax.broadcasted_iota(jnp.int32, kr_even.shape, 1) - QK_HEAD
        later_chunk = (c >= 0) & (c < kr_even.shape[0] // CHUNK - 1) & (row >> CHUNK_SHIFT > c)
        kr_even = jnp.where(later_chunk, 1.0, kr_even)
    kr_odd = pltpu.roll(kr_even, ODD_SHIFT, 1)
    kn = _dot(ckvn, wuk_ref[...])
    low_half = lax.broadcasted_iota(jnp.int32, kr_even.shape, 1) < QK_NOPE
    for hd in range(0, MLA_HEADS, 2):
        pair = kn[:, hd * QK_NOPE:(hd + 2) * QK_NOPE]
        k_ref[0, hd] = _bf16(jnp.where(low_half, pair, kr_even))
        k_ref[0, hd + 1] = _bf16(jnp.where(low_half, kr_odd, pair))

    vt = _dot_nt(wuv_ref[...], ckvn)
    ones_row = lax.broadcasted_iota(jnp.int32, (BF16_ROWS, vt.shape[1]), 0) == 0
    for hd in range(MLA_HEADS):
        vt_ref[0, 0, hd * V_SLOT:hd * V_SLOT + V_HEAD, :] = _bf16(vt[_head_rows(hd, V_HEAD)])
        vt_ref[0, 0, hd * V_SLOT + V_HEAD:(hd + 1) * V_SLOT, :] = ones_row.astype(jnp.bfloat16)

    scale = QK_HEAD ** -0.5 * LOG2_E
    qt = _dot_nt(wuq_ref[...], cqn) * scale
    for hd in range(MLA_HEADS):
        base = hd * QK_HEAD
        nope = _bf16(qt[base:base + QK_NOPE])
        x1 = qt[base + QK_NOPE:base + QK_NOPE + HALF_ROPE]
        x2 = qt[base + QK_NOPE + HALF_ROPE:base + QK_HEAD]
        rope = _bf16(jnp.concatenate([x1 * cos_t - x2 * sin_t, x1 * sin_t + x2 * cos_t], axis=0))
        if hd % 2 == 0:
            qt_ref[0, base:base + QK_NOPE, :] = nope
            qt_ref[0, base + QK_NOPE:base + QK_HEAD, :] = rope
        else:
            qt_ref[0, base:base + QK_ROPE, :] = rope
            qt_ref[0, base + QK_ROPE:base + QK_HEAD, :] = nope


def _mla_proj_kernel(h_ref, cos_ref, sin_ref, hm_ref, cosm_ref, sinm_ref,
                     w1_ref, gq_ref, gkv_ref, wuq_ref, wuk_ref, wuv_ref,
                     qt_ref, k_ref, vt_ref, gate_ref, qtm_ref, km_ref, vtm_ref, gatem_ref):
    weights = (w1_ref, gq_ref, gkv_ref, wuq_ref, wuk_ref, wuv_ref)

    @pl.when(_first_step())
    def _():
        _mla_proj_tile(hm_ref, cosm_ref, sinm_ref, *weights, qtm_ref, km_ref, vtm_ref, gatem_ref,
                       causal_lanes=False)

    _mla_proj_tile(h_ref, cos_ref, sin_ref, *weights, qt_ref, k_ref, vt_ref, gate_ref,
                   causal_lanes=True)


def _mla_proj(h, hm, tabs, meta_tabs, w):
    bx, lx, _ = h.shape
    tm = TOKEN_TILE
    nt = lx // tm
    tile = lambda b, i: (b, i, 0)
    tile_t = lambda b, i: (b, 0, i)

    def outputs(nb, n_tok, n_tiles, t):
        return [
            jax.ShapeDtypeStruct((nb, QK_WIDTH, n_tok), jnp.bfloat16),
            jax.ShapeDtypeStruct((nb, MLA_HEADS, n_tok, HEAD_SLOT), jnp.bfloat16),
            jax.ShapeDtypeStruct((nb, n_tiles, V_WIDTH, t), jnp.bfloat16),
            jax.ShapeDtypeStruct((nb, n_tok, MLA_WIDTH), jnp.bfloat16),
        ]

    res = pl.pallas_call(
        _mla_proj_kernel,
        grid=(bx, nt),
        in_specs=[
            pl.BlockSpec((1, tm, D_MODEL), tile),
            pl.BlockSpec((1, HALF_ROPE, tm), tile_t),
            pl.BlockSpec((1, HALF_ROPE, tm), tile_t),
            _const_spec((1, META_PAD, D_MODEL)),
            _const_spec((1, HALF_ROPE, META_PAD)),
            _const_spec((1, HALF_ROPE, META_PAD)),
            _const_spec((D_MODEL, _W1_COLS)),
            _const_spec((1, Q_LORA)),
            _const_spec((1, KV_LORA)),
            _const_spec((QK_WIDTH, Q_LORA)),
            _const_spec((KV_LORA, MLA_HEADS * QK_NOPE)),
            _const_spec((MLA_WIDTH, KV_LORA)),
        ],
        out_specs=[
            pl.BlockSpec((1, QK_WIDTH, tm), tile_t),
            pl.BlockSpec((1, MLA_HEADS, tm, HEAD_SLOT), lambda b, i: (b, 0, i, 0)),
            pl.BlockSpec((1, 1, V_WIDTH, tm), lambda b, i: (b, i, 0, 0)),
            pl.BlockSpec((1, tm, MLA_WIDTH), tile),
            _const_spec((1, QK_WIDTH, META_PAD)),
            _const_spec((1, MLA_HEADS, META_PAD, HEAD_SLOT)),
            _const_spec((1, 1, V_WIDTH, META_PAD)),
            _const_spec((1, META_PAD, MLA_WIDTH)),
        ],
        out_shape=outputs(bx, lx, nt, tm) + outputs(1, META_PAD, 1, META_PAD),
        compiler_params=pltpu.CompilerParams(
            dimension_semantics=("arbitrary", "arbitrary"), vmem_limit_bytes=VMEM_LIMIT),
        name="mla_proj",
    )(h, *tabs, hm, *meta_tabs, w["w1"], w["gq"], w["gkv"], w["wuq_t"], w["wuk"], w["wuv_t"])
    return res[:4], res[4:]


def _softmax_start(s):
    m = jnp.max(s, axis=0, keepdims=True)
    return m, jnp.exp2(s - m)


def _normalize(acc):
    return acc[:V_HEAD] / acc[V_HEAD:V_HEAD + 1]


def _q_operand(q_ref, hd, mask):
    q = q_ref[0, _head_rows(hd, QK_HEAD), :]
    zero = jnp.zeros((BF16_ROWS, q.shape[1]), q.dtype)
    spare = [zero if mask is None else mask, zero]
    parts = [q] + spare if hd % 2 == 0 else [q[:QK_ROPE]] + spare + [q[QK_ROPE:]]
    return jnp.concatenate(parts, axis=0)


def _attn_kernel(qt_ref, qn_ref, kb_ref, vb_ref, qtm_ref, km_ref, vm_ref, ot_ref, otm_ref,
                 kbuf_ref, vbuf_ref, s_ref, s0_ref, sm0_ref, m_ref, acc_ref):
    tk = vb_ref.shape[3]
    tq = qt_ref.shape[2]
    qi = pl.program_id(1)

    kbuf_ref[:, pl.ds(pl.multiple_of(qi * tk, tk), tk), :] = kb_ref[0]
    vbuf_ref[qi] = vb_ref[0, 0]

    row = lax.broadcasted_iota(jnp.int32, (BF16_ROWS, tq), 0)
    qry = lax.broadcasted_iota(jnp.int32, (BF16_ROWS, tq), 1)
    hide = (qry >> CHUNK_SHIFT == row) & (row < tk // CHUNK - 1)

    def mask_rows(is_diagonal):
        return _bf16(jnp.where(hide & is_diagonal, NEG_INF, 0.0))

    def scores(q_ref, hd, j, mask):
        row0 = pl.multiple_of(j * tk, tk)
        return _dot(kbuf_ref[hd, pl.ds(row0, tk), :], _q_operand(q_ref, hd, mask))

    def update(hd, s, j):
        m = m_ref[hd]
        m_new = jnp.maximum(m, jnp.max(s, axis=0, keepdims=True))
        alpha = jnp.exp2(m - m_new)
        p = jnp.exp2(s - m_new)
        m_ref[hd] = m_new
        acc_ref[hd] = alpha * acc_ref[hd] + _dot(vbuf_ref[j, _head_rows(hd, V_SLOT), :], _bf16(p))

    def meta_scores(q_ref):
        return [_dot(km_ref[0, hd], _q_operand(q_ref, hd, None))
                for hd in range(MLA_HEADS)]

    def meta_values(hd, p):
        no_weight = jnp.zeros((META_PAD - N_META, p.shape[1]), jnp.bfloat16)
        return _dot(vm_ref[0, 0, _head_rows(hd, V_SLOT), :],
                    jnp.concatenate([_bf16(p), no_weight], axis=0))

    @pl.when(_first_step())
    def _():
        sm = meta_scores(qtm_ref)
        for hd in range(MLA_HEADS):
            _, p = _softmax_start(sm[hd])
            otm_ref[0, _head_rows(hd, V_HEAD), :] = _bf16(_normalize(meta_values(hd, p)))

    @pl.when(qi == 0)
    def _():
        sm = meta_scores(qt_ref)
        diag_mask = mask_rows(True)
        for hd in range(MLA_HEADS):
            s_ref[0, hd] = scores(qt_ref, hd, 0, diag_mask)
            m, p = _softmax_start(sm[hd])
            m_ref[hd] = m
            acc_ref[hd] = meta_values(hd, p)

    @pl.when(qi > 0)
    def _():
        first_mask = mask_rows(qi == 1)
        for hd in range(MLA_HEADS):
            s_ref[1, hd] = scores(qt_ref, hd, 1, first_mask)
            sm = sm0_ref[hd]
            s = s0_ref[hd]
            m = jnp.maximum(jnp.max(sm, axis=0, keepdims=True), jnp.max(s, axis=0, keepdims=True))
            m_ref[hd] = m
            acc_ref[hd] = meta_values(hd, jnp.exp2(sm - m)) + _dot(
                vbuf_ref[0, _head_rows(hd, V_SLOT), :], _bf16(jnp.exp2(s - m)))

    def pair_step(jj, carry):
        j = 2 * jj + 1
        for hd in range(MLA_HEADS):
            s_ref[0, hd] = scores(qt_ref, hd, j + 1, None)
            update(hd, s_ref[1, hd], j)
        last_mask = mask_rows(j + 2 == qi)
        for hd in range(MLA_HEADS):
            s_ref[1, hd] = scores(qt_ref, hd, j + 2, last_mask)
            update(hd, s_ref[0, hd], j + 1)
        return carry

    lax.fori_loop(0, (qi - 1) // 2, pair_step, 0)

    @pl.when((qi >= 2) & (qi % 2 == 0))
    def _():
        diag_mask = mask_rows(True)
        for hd in range(MLA_HEADS):
            s_ref[0, hd] = scores(qt_ref, hd, qi, diag_mask)
            update(hd, s_ref[1, hd], qi - 1)

    for hd in range(MLA_HEADS):
        keys = jnp.concatenate([kbuf_ref[hd, 0:tk, :], km_ref[0, hd]], axis=0)
        s_next = _dot(keys, _q_operand(qn_ref, hd, None))
        s0_ref[hd] = s_next[:tk]
        sm0_ref[hd] = s_next[tk:]
        update(hd, s_ref[qi % 2, hd], qi)
    for hd in range(MLA_HEADS):
        ot_ref[0, _head_rows(hd, V_HEAD), :] = _bf16(_normalize(acc_ref[hd]))


def _attention(frames, meta):
    qt, k, vt, _ = frames
    qt_m, k_m, vt_m, _ = meta
    bx, _, lx = qt.shape
    nk, tk = vt.shape[1], vt.shape[3]
    tq = tk
    nq = lx // tq
    return pl.pallas_call(
        _attn_kernel,
        grid=(bx, nq),
        in_specs=[
            pl.BlockSpec((1, QK_WIDTH, tq), lambda b, i: (b, 0, i)),
            pl.BlockSpec((1, QK_WIDTH, tq), lambda b, i: (b, 0, jnp.minimum(i + 1, nq - 1))),
            pl.BlockSpec((1, MLA_HEADS, tk, HEAD_SLOT), lambda b, i: (b, 0, i, 0)),
            pl.BlockSpec((1, 1, V_WIDTH, tk), lambda b, i: (b, i, 0, 0)),
            _const_spec((1, QK_WIDTH, META_PAD)),
            _const_spec((1, MLA_HEADS, N_META, HEAD_SLOT)),
            _const_spec((1, 1, V_WIDTH, META_PAD)),
        ],
        out_specs=[
            pl.BlockSpec((1, MLA_WIDTH, tq), lambda b, i: (b, 0, i)),
            _const_spec((1, MLA_WIDTH, META_PAD)),
        ],
        out_shape=[
            jax.ShapeDtypeStruct((bx, MLA_WIDTH, lx), jnp.bfloat16),
            jax.ShapeDtypeStruct((1, MLA_WIDTH, META_PAD), jnp.bfloat16),
        ],
        scratch_shapes=[
            pltpu.VMEM((MLA_HEADS, lx, HEAD_SLOT), jnp.bfloat16),
            pltpu.VMEM((nk, V_WIDTH, tk), jnp.bfloat16),
            pltpu.VMEM((2, MLA_HEADS, tk, tq), jnp.float32),
            pltpu.VMEM((MLA_HEADS, tk, tq), jnp.float32),
            pltpu.VMEM((MLA_HEADS, N_META, tq), jnp.float32),
            pltpu.VMEM((MLA_HEADS, 1, tq), jnp.float32),
            pltpu.VMEM((MLA_HEADS, V_SLOT, tq), jnp.float32),
        ],
        compiler_params=pltpu.CompilerParams(
            dimension_semantics=("arbitrary", "arbitrary"), vmem_limit_bytes=VMEM_LIMIT),
        name="attention",
    )(qt, qt, k, vt, qt_m, k_m, vt_m)


def _out_conv_tile(ot_ref, gate_ref, h_ref, out_ref, cu_ref,
                   wo_ref, g1_ref, b1_ref, win_ref, cw_ref, wout_ref, g2_ref, b2_ref, chunk):
    tm = h_ref.shape[1]
    w = CONV_WIDTH
    chunks = [slice(c0, c0 + chunk) for c0 in range(0, tm, chunk)]

    attn = []
    for rows in chunks:
        o = ot_ref[0, :, rows].astype(jnp.float32).T
        y = _bf16(o * gate_ref[0, rows, :].astype(jnp.float32))
        attn.append(_dot(y, wo_ref[...]))

    hs, ys = [], []
    for rows, a in zip(chunks, attn):
        h = _layer_norm(DN_ALPHA * h_ref[0, rows, :] + a, g1_ref[...], b1_ref[...])
        hb = _bf16(h)
        cu = _dot(hb, win_ref[:, w:2 * w]) * _dot(hb, win_ref[:, 2 * w:3 * w])
        c0 = rows.start
        cu_ref[SUBLANES + c0:SUBLANES + c0 + chunk, :] = cu
        conv = cu_ref[SUBLANES - 2 + c0:SUBLANES - 2 + c0 + chunk, :] * cw_ref[0:1, :]
        conv = conv + cu_ref[SUBLANES - 1 + c0:SUBLANES - 1 + c0 + chunk, :] * cw_ref[1:2, :]
        conv = conv + cu * cw_ref[2:3, :]
        hs.append(h)
        ys.append(_dot(hb, win_ref[:, 0:w]) * conv * _silu(_dot(hb, win_ref[:, 3 * w:4 * w])))

    for rows, h, y in zip(chunks, hs, ys):
        r = DN_ALPHA * h + _dot(_bf16(y), wout_ref[...])
        out_ref[0, rows, :] = _layer_norm(r, g2_ref[...], b2_ref[...])


def _out_conv_kernel(ot_ref, gate_ref, h_ref, otm_ref, gatem_ref, hm_ref,
                     wo_ref, g1_ref, b1_ref, win_ref, cw_ref, wout_ref, g2_ref, b2_ref,
                     out_ref, outm_ref, cu_ref, meta_tail_ref):
    tm = h_ref.shape[1]
    weights = (wo_ref, g1_ref, b1_ref, win_ref, cw_ref, wout_ref, g2_ref, b2_ref)

    @pl.when(_first_step())
    def _():
        cu_ref[0:SUBLANES, :] = jnp.zeros((SUBLANES, CONV_WIDTH), jnp.float32)
        _out_conv_tile(otm_ref, gatem_ref, hm_ref, outm_ref, cu_ref, *weights, chunk=META_PAD)
        meta_tail_ref[...] = cu_ref[N_META:N_META + SUBLANES, :]

    @pl.when(pl.program_id(1) == 0)
    def _():
        cu_ref[0:SUBLANES, :] = meta_tail_ref[...]

    _out_conv_tile(ot_ref, gate_ref, h_ref, out_ref, cu_ref, *weights, chunk=TOKEN_TILE)
    cu_ref[0:SUBLANES, :] = cu_ref[tm:tm + SUBLANES, :]


def _out_conv(ot, gate, h, ot_m, gate_m, hm, w):
    bx, lx, _ = h.shape
    tm = CONV_TILE
    tile = lambda b_, i: (b_, i, 0)
    once = pl.Buffered(1)
    return pl.pallas_call(
        _out_conv_kernel,
        grid=(bx, lx // tm),
        in_specs=[
            pl.BlockSpec((1, MLA_WIDTH, tm), lambda b_, i: (b_, 0, i)),
            pl.BlockSpec((1, tm, MLA_WIDTH), tile),
            pl.BlockSpec((1, tm, D_MODEL), tile),
            _const_spec((1, MLA_WIDTH, META_PAD), once),
            _const_spec((1, META_PAD, MLA_WIDTH), once),
            _const_spec((1, META_PAD, D_MODEL), once),
            _const_spec((MLA_WIDTH, D_MODEL), once),
            _const_spec((1, D_MODEL), once),
            _const_spec((1, D_MODEL), once),
            _const_spec((D_MODEL, 4 * CONV_WIDTH), once),
            _const_spec((SUBLANES, CONV_WIDTH), once),
            _const_spec((CONV_WIDTH, D_MODEL), once),
            _const_spec((1, D_MODEL), once),
            _const_spec((1, D_MODEL), once),
        ],
        out_specs=[
            pl.BlockSpec((1, tm, D_MODEL), tile),
            _const_spec((1, META_PAD, D_MODEL)),
        ],
        out_shape=[
            jax.ShapeDtypeStruct(h.shape, jnp.float32),
            jax.ShapeDtypeStruct(hm.shape, jnp.float32),
        ],
        scratch_shapes=[
            pltpu.VMEM((tm + SUBLANES, CONV_WIDTH), jnp.float32),
            pltpu.VMEM((SUBLANES, CONV_WIDTH), jnp.float32),
        ],
        compiler_params=pltpu.CompilerParams(
            dimension_semantics=("arbitrary", "arbitrary"), vmem_limit_bytes=VMEM_LIMIT),
        name="out_conv",
    )(ot, gate, h, ot_m, gate_m, hm,
      w["wo"], w["g1"], w["b1"], w["win"], w["cw"], w["wout"], w["g2"], w["b2"])


def _prep_weights(ln_g, ln_b, w_in, gq, w_uq, gkv, w_uk, w_uv, w_o, conv_w_in, conv_w, conv_w_out):
    n = w_in.shape[0]
    o_kr = Q_LORA + KV_LORA
    o_z = o_kr + QK_ROPE
    b = _bf16(w_in)
    kr1 = b[:, :, o_kr:o_kr + HALF_ROPE]
    kr2 = b[:, :, o_kr + HALF_ROPE:o_z]
    zl = jnp.zeros((n, D_MODEL, QK_NOPE), jnp.bfloat16)
    stacked = {
        "w1": jnp.concatenate([b[:, :, :o_kr], b[:, :, o_z:], zl, kr1, kr2, -kr2, kr1], axis=2),
        "gq": gq.reshape(n, 1, Q_LORA),
        "gkv": gkv.reshape(n, 1, KV_LORA),
        "wuq_t": jnp.swapaxes(_bf16(w_uq), 1, 2),
        "wuk": _bf16(w_uk),
        "wuv_t": jnp.swapaxes(_bf16(w_uv), 1, 2),
        "wo": _bf16(w_o),
        "g1": ln_g[0::2].reshape(n, 1, D_MODEL), "b1": ln_b[0::2].reshape(n, 1, D_MODEL),
        "g2": ln_g[1::2].reshape(n, 1, D_MODEL), "b2": ln_b[1::2].reshape(n, 1, D_MODEL),
        "win": _bf16(conv_w_in), "wout": _bf16(conv_w_out),
        "cw": jnp.pad(conv_w.astype(jnp.float32), ((0, 0), (0, SUBLANES - CONV_K), (0, 0))),
    }
    return [{name: v[j] for name, v in stacked.items()} for j in range(n)]


def _rope_tables(pos):
    inv_freq = ROPE_BASE ** (-jnp.arange(0, QK_ROPE, 2, dtype=jnp.float32) / QK_ROPE)
    ang = pos[..., None] * inv_freq
    return jnp.swapaxes(jnp.cos(ang), 1, 2), jnp.swapaxes(jnp.sin(ang), 1, 2)


def kernel(x, positions, meta_tokens, ln_g, ln_b, mla_w_in, mla_q_norm_g, mla_w_uq, mla_kv_norm_g,
           mla_w_uk, mla_w_uv, mla_w_o, conv_w_in, conv_w, conv_w_out):
    bsz, seq, _ = x.shape
    assert seq % CONV_TILE == 0 and CONV_TILE % TOKEN_TILE == 0 and TOKEN_TILE % CHUNK == 0
    assert DEPTH % 2 == 0
    f32 = jnp.float32

    hf = x.astype(f32)
    hm = jnp.pad(meta_tokens.astype(f32), ((0, META_PAD - N_META), (0, 0)))[None]

    frame_tabs = _rope_tables((positions + N_META).astype(f32))
    meta_pos = jnp.pad(jnp.arange(N_META, dtype=f32), (0, META_PAD - N_META))[None]
    meta_tabs = _rope_tables(meta_pos)

    weights = _prep_weights(ln_g, ln_b, mla_w_in, mla_q_norm_g, mla_w_uq, mla_kv_norm_g,
                            mla_w_uk, mla_w_uv, mla_w_o, conv_w_in, conv_w, conv_w_out)
    for w in weights:
        frames, meta = _mla_proj(hf, hm, frame_tabs, meta_tabs, w)
        ot_f, ot_m = _attention(frames, meta)
        hf, hm = _out_conv(ot_f, frames[3], hf, ot_m, meta[3], hm, w)
    return hf
```

```python
import jax
import jax.numpy as jnp
from jax import lax
from jax.experimental import pallas as pl
from jax.experimental.pallas import tpu as pltpu

D_MODEL = 1024
DEPTH = 4
CHUNK = 64
CHUNK_SHIFT = CHUNK.bit_length() - 1
N_META = 16
MLA_HEADS = 16
QK_NOPE = 64
QK_ROPE = 32
HALF_ROPE = QK_ROPE // 2
V_HEAD = 64
Q_LORA = 384
KV_LORA = 256
MLA_WIDTH = MLA_HEADS * V_HEAD
ROPE_BASE = 10000.0
CONV_WIDTH = D_MODEL
CONV_K = 3
DN_ALPHA = (2 * DEPTH) ** 0.25
LN_EPS = 1e-5
RMS_EPS = 1e-6
NEG_INF = -1e30
LOG2_E = 1.4426950408889634

QK_HEAD = QK_NOPE + QK_ROPE
QK_WIDTH = MLA_HEADS * QK_HEAD
HEAD_SLOT = 128
ODD_SHIFT = HEAD_SLOT - QK_NOPE
BF16_ROWS = 16
V_SLOT = V_HEAD + BF16_ROWS
V_WIDTH = MLA_HEADS * V_SLOT
META_PAD = 128
TOKEN_TILE = 256
CONV_TILE = 512
PROJ_TILE = 512
SUBLANES = 8
VMEM_LIMIT = 56 * 1024 * 1024

_OFF_CKV = Q_LORA
_OFF_Z = Q_LORA + KV_LORA
_OFF_ROPE = _OFF_Z + MLA_WIDTH
_W1_COLS = _OFF_ROPE + HEAD_SLOT

_NT = (((1,), (1,)), ((), ()))


def _bf16(x):
    return x.astype(jnp.bfloat16)


def _dot(a, b):
    return jnp.dot(a, b, preferred_element_type=jnp.float32)


def _dot_nt(a, b):
    return lax.dot_general(a, b, _NT, preferred_element_type=jnp.float32)


def _rms(x, g):
    y = x * lax.rsqrt(jnp.mean(jnp.square(x), axis=-1, keepdims=True) + RMS_EPS)
    return y * g


def _layer_norm(x, g, b):
    mu = jnp.mean(x, axis=-1, keepdims=True)
    xc = x - mu
    var = jnp.mean(jnp.square(xc), axis=-1, keepdims=True)
    return xc * lax.rsqrt(var + LN_EPS) * g + b


def _silu(z):
    return z * (1.0 / (1.0 + jnp.exp(-z)))


def _head_rows(hd, width):
    return slice(hd * width, (hd + 1) * width)


def _first_step():
    return (pl.program_id(0) == 0) & (pl.program_id(1) == 0)


def _const_spec(shape, pipeline_mode=None):
    return pl.BlockSpec(shape, lambda b, i: (0,) * len(shape), pipeline_mode=pipeline_mode)


def _mla_proj_tile(h_ref, cos_t_ref, sin_t_ref, w1_ref, gq_ref, gkv_ref, wuq_ref, wuk_ref, wuv_ref,
                   qt_ref, k_ref, vt_ref, gate_ref, causal_lanes, chunk):
    tm = h_ref.shape[1]
    chunks = [slice(c0, c0 + chunk) for c0 in range(0, tm, chunk)]

    def slot_table(t):
        rows = [jnp.zeros((QK_NOPE, chunk), t.dtype), t, t,
                jnp.zeros((HEAD_SLOT - QK_HEAD, chunk), t.dtype)]
        return jnp.concatenate(rows, axis=0).T

    ps = [_dot(_bf16(h_ref[0, rows, :]), w1_ref[...]) for rows in chunks]

    lora = []
    for rows, p in zip(chunks, ps):
        cqn = _bf16(_rms(p[:, :Q_LORA], gq_ref[...]))
        ckvn = _bf16(_rms(p[:, _OFF_CKV:_OFF_Z], gkv_ref[...]))
        gate_ref[0, rows, :] = _bf16(_silu(p[:, _OFF_Z:_OFF_ROPE]))
        cos_t = cos_t_ref[0, :, rows]
        sin_t = sin_t_ref[0, :, rows]
        lora.append((cqn, ckvn, cos_t, sin_t))

        ab = p[:, _OFF_ROPE:_W1_COLS]
        kr_even = ab * slot_table(cos_t) + pltpu.roll(ab, HEAD_SLOT - QK_ROPE, 1) * slot_table(sin_t)
        if causal_lanes:
            row = lax.broadcasted_iota(jnp.int32, kr_even.shape, 0)
            c = lax.broadcasted_iota(jnp.int32, kr_even.shape, 1) - QK_HEAD
            later_chunk = (c >= 0) & (c < chunk // CHUNK - 1) & (row >> CHUNK_SHIFT > c)
            kr_even = jnp.where(later_chunk, 1.0, kr_even)
        kr_odd = pltpu.roll(kr_even, ODD_SHIFT, 1)
        kn = _dot(ckvn, wuk_ref[...])
        low_half = lax.broadcasted_iota(jnp.int32, kr_even.shape, 1) < QK_NOPE
        for hd in range(0, MLA_HEADS, 2):
            pair = kn[:, hd * QK_NOPE:(hd + 2) * QK_NOPE]
            k_ref[0, hd, rows, :] = _bf16(jnp.where(low_half, pair, kr_even))
            k_ref[0, hd + 1, rows, :] = _bf16(jnp.where(low_half, kr_odd, pair))

    ones_row = (lax.broadcasted_iota(jnp.int32, (BF16_ROWS, chunk), 0) == 0).astype(jnp.bfloat16)
    for ci, (_, ckvn, _, _) in enumerate(lora):
        vt = _dot_nt(wuv_ref[...], ckvn)
        for hd in range(MLA_HEADS):
            vt_ref[0, ci, hd * V_SLOT:hd * V_SLOT + V_HEAD, :] = _bf16(vt[_head_rows(hd, V_HEAD)])
            vt_ref[0, ci, hd * V_SLOT + V_HEAD:(hd + 1) * V_SLOT, :] = ones_row

    scale = QK_HEAD ** -0.5 * LOG2_E
    for rows, (cqn, _, cos_t, sin_t) in zip(chunks, lora):
        qt = _dot_nt(wuq_ref[...], cqn) * scale
        for hd in range(MLA_HEADS):
            base = hd * QK_HEAD
            nope = _bf16(qt[base:base + QK_NOPE])
            x1 = qt[base + QK_NOPE:base + QK_NOPE + HALF_ROPE]
            x2 = qt[base + QK_NOPE + HALF_ROPE:base + QK_HEAD]
            rope = _bf16(jnp.concatenate([x1 * cos_t - x2 * sin_t, x1 * sin_t + x2 * cos_t], axis=0))
            if hd % 2 == 0:
                qt_ref[0, base:base + QK_NOPE, rows] = nope
                qt_ref[0, base + QK_NOPE:base + QK_HEAD, rows] = rope
            else:
                qt_ref[0, base:base + QK_ROPE, rows] = rope
                qt_ref[0, base + QK_ROPE:base + QK_HEAD, rows] = nope


def _mla_proj_kernel(h_ref, cos_ref, sin_ref, hm_ref, cosm_ref, sinm_ref,
                     w1_ref, gq_ref, gkv_ref, wuq_ref, wuk_ref, wuv_ref,
                     qt_ref, k_ref, vt_ref, gate_ref, qtm_ref, km_ref, vtm_ref, gatem_ref):
    weights = (w1_ref, gq_ref, gkv_ref, wuq_ref, wuk_ref, wuv_ref)

    @pl.when(_first_step())
    def _():
        _mla_proj_tile(hm_ref, cosm_ref, sinm_ref, *weights, qtm_ref, km_ref, vtm_ref, gatem_ref,
                       causal_lanes=False, chunk=META_PAD)

    _mla_proj_tile(h_ref, cos_ref, sin_ref, *weights, qt_ref, k_ref, vt_ref, gate_ref,
                   causal_lanes=True, chunk=TOKEN_TILE)


def _mla_proj(h, hm, tabs, meta_tabs, w):
    bx, lx, _ = h.shape
    tm = PROJ_TILE
    per_step = tm // TOKEN_TILE
    tile = lambda b, i: (b, i, 0)
    tile_t = lambda b, i: (b, 0, i)

    def outputs(nb, n_tok, n_tiles, t):
        return [
            jax.ShapeDtypeStruct((nb, QK_WIDTH, n_tok), jnp.bfloat16),
            jax.ShapeDtypeStruct((nb, MLA_HEADS, n_tok, HEAD_SLOT), jnp.bfloat16),
            jax.ShapeDtypeStruct((nb, n_tiles, V_WIDTH, t), jnp.bfloat16),
            jax.ShapeDtypeStruct((nb, n_tok, MLA_WIDTH), jnp.bfloat16),
        ]

    res = pl.pallas_call(
        _mla_proj_kernel,
        grid=(bx, lx // tm),
        in_specs=[
            pl.BlockSpec((1, tm, D_MODEL), tile),
            pl.BlockSpec((1, HALF_ROPE, tm), tile_t),
            pl.BlockSpec((1, HALF_ROPE, tm), tile_t),
            _const_spec((1, META_PAD, D_MODEL)),
            _const_spec((1, HALF_ROPE, META_PAD)),
            _const_spec((1, HALF_ROPE, META_PAD)),
            _const_spec((D_MODEL, _W1_COLS)),
            _const_spec((1, Q_LORA)),
            _const_spec((1, KV_LORA)),
            _const_spec((QK_WIDTH, Q_LORA)),
            _const_spec((KV_LORA, MLA_HEADS * QK_NOPE)),
            _const_spec((MLA_WIDTH, KV_LORA)),
        ],
        out_specs=[
            pl.BlockSpec((1, QK_WIDTH, tm), tile_t),
            pl.BlockSpec((1, MLA_HEADS, tm, HEAD_SLOT), lambda b, i: (b, 0, i, 0)),
            pl.BlockSpec((1, per_step, V_WIDTH, TOKEN_TILE), lambda b, i: (b, i, 0, 0)),
            pl.BlockSpec((1, tm, MLA_WIDTH), tile),
            _const_spec((1, QK_WIDTH, META_PAD)),
            _const_spec((1, MLA_HEADS, META_PAD, HEAD_SLOT)),
            _const_spec((1, 1, V_WIDTH, META_PAD)),
            _const_spec((1, META_PAD, MLA_WIDTH)),
        ],
        out_shape=outputs(bx, lx, lx // TOKEN_TILE, TOKEN_TILE) + outputs(1, META_PAD, 1, META_PAD),
        compiler_params=pltpu.CompilerParams(
            dimension_semantics=("arbitrary", "arbitrary"), vmem_limit_bytes=VMEM_LIMIT),
        name="mla_proj",
    )(h, *tabs, hm, *meta_tabs, w["w1"], w["gq"], w["gkv"], w["wuq_t"], w["wuk"], w["wuv_t"])
    return res[:4], res[4:]


def _softmax_start(s):
    m = jnp.max(s, axis=0, keepdims=True)
    return m, jnp.exp2(s - m)


def _normalize(acc):
    return acc[:V_HEAD] / acc[V_HEAD:V_HEAD + 1]


def _q_operand(q_ref, hd, mask):
    q = q_ref[0, _head_rows(hd, QK_HEAD), :]
    zero = jnp.zeros((BF16_ROWS, q.shape[1]), q.dtype)
    spare = [zero if mask is None else mask, zero]
    parts = [q] + spare if hd % 2 == 0 else [q[:QK_ROPE]] + spare + [q[QK_ROPE:]]
    return jnp.concatenate(parts, axis=0)


def _attn_kernel(qt_ref, qn_ref, kb_ref, vb_ref, qtm_ref, km_ref, vm_ref, ot_ref, otm_ref,
                 kbuf_ref, vbuf_ref, s_ref, s0_ref, sm0_ref, m_ref, acc_ref):
    tk = vb_ref.shape[3]
    tq = qt_ref.shape[2]
    qi = pl.program_id(1)

    kbuf_ref[:, pl.ds(pl.multiple_of(qi * tk, tk), tk), :] = kb_ref[0]
    vbuf_ref[qi] = vb_ref[0, 0]

    row = lax.broadcasted_iota(jnp.int32, (BF16_ROWS, tq), 0)
    qry = lax.broadcasted_iota(jnp.int32, (BF16_ROWS, tq), 1)
    hide = (qry >> CHUNK_SHIFT == row) & (row < tk // CHUNK - 1)

    def mask_rows(is_diagonal):
        return _bf16(jnp.where(hide & is_diagonal, NEG_INF, 0.0))

    def scores(q_ref, hd, j, mask):
        row0 = pl.multiple_of(j * tk, tk)
        return _dot(kbuf_ref[hd, pl.ds(row0, tk), :], _q_operand(q_ref, hd, mask))

    def update(hd, s, j):
        m = m_ref[hd]
        m_new = jnp.maximum(m, jnp.max(s, axis=0, keepdims=True))
        alpha = jnp.exp2(m - m_new)
        p = jnp.exp2(s - m_new)
        m_ref[hd] = m_new
        acc_ref[hd] = alpha * acc_ref[hd] + _dot(vbuf_ref[j, _head_rows(hd, V_SLOT), :], _bf16(p))

    def meta_scores(q_ref):
        return [_dot(km_ref[0, hd], _q_operand(q_ref, hd, None))
                for hd in range(MLA_HEADS)]

    def meta_values(hd, p):
        no_weight = jnp.zeros((META_PAD - N_META, p.shape[1]), jnp.bfloat16)
        return _dot(vm_ref[0, 0, _head_rows(hd, V_SLOT), :],
                    jnp.concatenate([_bf16(p), no_weight], axis=0))

    @pl.when(_first_step())
    def _():
        sm = meta_scores(qtm_ref)
        for hd in range(MLA_HEADS):
            _, p = _softmax_start(sm[hd])
            otm_ref[0, _head_rows(hd, V_HEAD), :] = _bf16(_normalize(meta_values(hd, p)))

    @pl.when(qi == 0)
    def _():
        sm = meta_scores(qt_ref)
        diag_mask = mask_rows(True)
        for hd in range(MLA_HEADS):
            s_ref[0, hd] = scores(qt_ref, hd, 0, diag_mask)
            m, p = _softmax_start(sm[hd])
            m_ref[hd] = m
            acc_ref[hd] = meta_values(hd, p)

    @pl.when(qi > 0)
    def _():
        first_mask = mask_rows(qi == 1)
        for hd in range(MLA_HEADS):
            s_ref[1, hd] = scores(qt_ref, hd, 1, first_mask)
            sm = sm0_ref[hd]
            s = s0_ref[hd]
            m = jnp.maximum(jnp.max(sm, axis=0, keepdims=True), jnp.max(s, axis=0, keepdims=True))
            m_ref[hd] = m
            acc_ref[hd] = meta_values(hd, jnp.exp2(sm - m)) + _dot(
                vbuf_ref[0, _head_rows(hd, V_SLOT), :], _bf16(jnp.exp2(s - m)))

    def pair(j):
        for hd in range(MLA_HEADS):
            s_ref[0, hd] = scores(qt_ref, hd, j + 1, None)
            update(hd, s_ref[1, hd], j)
        last_mask = mask_rows(j + 2 == qi)
        for hd in range(MLA_HEADS):
            s_ref[1, hd] = scores(qt_ref, hd, j + 2, last_mask)
            update(hd, s_ref[0, hd], j + 1)

    def quad_step(jj, carry):
        pair(4 * jj + 1)
        pair(4 * jj + 3)
        return carry

    n_mid = jnp.maximum(qi - 1, 0)
    lax.fori_loop(0, n_mid // 4, quad_step, 0)

    @pl.when(n_mid % 4 >= 2)
    def _():
        pair(4 * (n_mid // 4) + 1)

    @pl.when((qi >= 2) & (qi % 2 == 0))
    def _():
        diag_mask = mask_rows(True)
        for hd in range(MLA_HEADS):
            s_ref[0, hd] = scores(qt_ref, hd, qi, diag_mask)
            update(hd, s_ref[1, hd], qi - 1)

    for hd in range(MLA_HEADS):
        keys = jnp.concatenate([kbuf_ref[hd, 0:tk, :], km_ref[0, hd]], axis=0)
        s_next = _dot(keys, _q_operand(qn_ref, hd, None))
        s0_ref[hd] = s_next[:tk]
        sm0_ref[hd] = s_next[tk:]
        update(hd, s_ref[qi % 2, hd], qi)
    for hd in range(MLA_HEADS):
        ot_ref[0, _head_rows(hd, V_HEAD), :] = _bf16(_normalize(acc_ref[hd]))


def _attention(frames, meta):
    qt, k, vt, _ = frames
    qt_m, k_m, vt_m, _ = meta
    bx, _, lx = qt.shape
    nk, tk = vt.shape[1], vt.shape[3]
    tq = tk
    nq = lx // tq
    return pl.pallas_call(
        _attn_kernel,
        grid=(bx, nq),
        in_specs=[
            pl.BlockSpec((1, QK_WIDTH, tq), lambda b, i: (b, 0, i)),
            pl.BlockSpec((1, QK_WIDTH, tq), lambda b, i: (b, 0, jnp.minimum(i + 1, nq - 1))),
            pl.BlockSpec((1, MLA_HEADS, tk, HEAD_SLOT), lambda b, i: (b, 0, i, 0)),
            pl.BlockSpec((1, 1, V_WIDTH, tk), lambda b, i: (b, i, 0, 0)),
            _const_spec((1, QK_WIDTH, META_PAD)),
            _const_spec((1, MLA_HEADS, N_META, HEAD_SLOT)),
            _const_spec((1, 1, V_WIDTH, META_PAD)),
        ],
        out_specs=[
            pl.BlockSpec((1, MLA_WIDTH, tq), lambda b, i: (b, 0, i)),
            _const_spec((1, MLA_WIDTH, META_PAD)),
        ],
        out_shape=[
            jax.ShapeDtypeStruct((bx, MLA_WIDTH, lx), jnp.bfloat16),
            jax.ShapeDtypeStruct((1, MLA_WIDTH, META_PAD), jnp.bfloat16),
        ],
        scratch_shapes=[
            pltpu.VMEM((MLA_HEADS, lx, HEAD_SLOT), jnp.bfloat16),
            pltpu.VMEM((nk, V_WIDTH, tk), jnp.bfloat16),
            pltpu.VMEM((2, MLA_HEADS, tk, tq), jnp.float32),
            pltpu.VMEM((MLA_HEADS, tk, tq), jnp.float32),
            pltpu.VMEM((MLA_HEADS, N_META, tq), jnp.float32),
            pltpu.VMEM((MLA_HEADS, 1, tq), jnp.float32),
            pltpu.VMEM((MLA_HEADS, V_SLOT, tq), jnp.float32),
        ],
        compiler_params=pltpu.CompilerParams(
            dimension_semantics=("arbitrary", "arbitrary"), vmem_limit_bytes=VMEM_LIMIT),
        name="attention",
    )(qt, qt, k, vt, qt_m, k_m, vt_m)


def _out_conv_tile(ot_ref, gate_ref, h_ref, out_ref, cu_ref,
                   wo_ref, g1_ref, b1_ref, win_ref, cw_ref, wout_ref, g2_ref, b2_ref, chunk):
    tm = h_ref.shape[1]
    w = CONV_WIDTH
    chunks = [slice(c0, c0 + chunk) for c0 in range(0, tm, chunk)]

    attn = []
    for rows in chunks:
        o = ot_ref[0, :, rows].astype(jnp.float32).T
        y = _bf16(o * gate_ref[0, rows, :].astype(jnp.float32))
        attn.append(_dot(y, wo_ref[...]))

    hs, ys = [], []
    for rows, a in zip(chunks, attn):
        h = _layer_norm(DN_ALPHA * h_ref[0, rows, :] + a, g1_ref[...], b1_ref[...])
        hb = _bf16(h)
        cu = _dot(hb, win_ref[:, w:2 * w]) * _dot(hb, win_ref[:, 2 * w:3 * w])
        c0 = rows.start
        cu_ref[SUBLANES + c0:SUBLANES + c0 + chunk, :] = cu
        conv = cu_ref[SUBLANES - 2 + c0:SUBLANES - 2 + c0 + chunk, :] * cw_ref[0:1, :]
        conv = conv + cu_ref[SUBLANES - 1 + c0:SUBLANES - 1 + c0 + chunk, :] * cw_ref[1:2, :]
        conv = conv + cu * cw_ref[2:3, :]
        hs.append(h)
        ys.append(_dot(hb, win_ref[:, 0:w]) * conv * _silu(_dot(hb, win_ref[:, 3 * w:4 * w])))

    for rows, h, y in zip(chunks, hs, ys):
        r = DN_ALPHA * h + _dot(_bf16(y), wout_ref[...])
        out_ref[0, rows, :] = _layer_norm(r, g2_ref[...], b2_ref[...])


def _out_conv_kernel(ot_ref, gate_ref, h_ref, otm_ref, gatem_ref, hm_ref,
                     wo_ref, g1_ref, b1_ref, win_ref, cw_ref, wout_ref, g2_ref, b2_ref,
                     out_ref, outm_ref, cu_ref, meta_tail_ref):
    tm = h_ref.shape[1]
    weights = (wo_ref, g1_ref, b1_ref, win_ref, cw_ref, wout_ref, g2_ref, b2_ref)

    @pl.when(_first_step())
    def _():
        cu_ref[0:SUBLANES, :] = jnp.zeros((SUBLANES, CONV_WIDTH), jnp.float32)
        _out_conv_tile(otm_ref, gatem_ref, hm_ref, outm_ref, cu_ref, *weights, chunk=META_PAD)
        meta_tail_ref[...] = cu_ref[N_META:N_META + SUBLANES, :]

    @pl.when(pl.program_id(1) == 0)
    def _():
        cu_ref[0:SUBLANES, :] = meta_tail_ref[...]

    _out_conv_tile(ot_ref, gate_ref, h_ref, out_ref, cu_ref, *weights, chunk=TOKEN_TILE)
    cu_ref[0:SUBLANES, :] = cu_ref[tm:tm + SUBLANES, :]


def _out_conv(ot, gate, h, ot_m, gate_m, hm, w):
    bx, lx, _ = h.shape
    tm = CONV_TILE
    tile = lambda b_, i: (b_, i, 0)
    once = pl.Buffered(1)
    return pl.pallas_call(
        _out_conv_kernel,
        grid=(bx, lx // tm),
        in_specs=[
            pl.BlockSpec((1, MLA_WIDTH, tm), lambda b_, i: (b_, 0, i)),
            pl.BlockSpec((1, tm, MLA_WIDTH), tile),
            pl.BlockSpec((1, tm, D_MODEL), tile),
            _const_spec((1, MLA_WIDTH, META_PAD), once),
            _const_spec((1, META_PAD, MLA_WIDTH), once),
            _const_spec((1, META_PAD, D_MODEL), once),
            _const_spec((MLA_WIDTH, D_MODEL), once),
            _const_spec((1, D_MODEL), once),
            _const_spec((1, D_MODEL), once),
            _const_spec((D_MODEL, 4 * CONV_WIDTH), once),
            _const_spec((SUBLANES, CONV_WIDTH), once),
            _const_spec((CONV_WIDTH, D_MODEL), once),
            _const_spec((1, D_MODEL), once),
            _const_spec((1, D_MODEL), once),
        ],
        out_specs=[
            pl.BlockSpec((1, tm, D_MODEL), tile),
            _const_spec((1, META_PAD, D_MODEL)),
        ],
        out_shape=[
            jax.ShapeDtypeStruct(h.shape, jnp.float32),
            jax.ShapeDtypeStruct(hm.shape, jnp.float32),
        ],
        scratch_shapes=[
            pltpu.VMEM((tm + SUBLANES, CONV_WIDTH), jnp.float32),
            pltpu.VMEM((SUBLANES, CONV_WIDTH), jnp.float32),
        ],
        compiler_params=pltpu.CompilerParams(
            dimension_semantics=("arbitrary", "arbitrary"), vmem_limit_bytes=VMEM_LIMIT),
        name="out_conv",
    )(ot, gate, h, ot_m, gate_m, hm,
      w["wo"], w["g1"], w["b1"], w["win"], w["cw"], w["wout"], w["g2"], w["b2"])


def _prep_weights(ln_g, ln_b, w_in, gq, w_uq, gkv, w_uk, w_uv, w_o, conv_w_in, conv_w, conv_w_out):
    n = w_in.shape[0]
    o_kr = Q_LORA + KV_LORA
    o_z = o_kr + QK_ROPE
    b = _bf16(w_in)
    kr1 = b[:, :, o_kr:o_kr + HALF_ROPE]
    kr2 = b[:, :, o_kr + HALF_ROPE:o_z]
    zl = jnp.zeros((n, D_MODEL, QK_NOPE), jnp.bfloat16)
    stacked = {
        "w1": jnp.concatenate([b[:, :, :o_kr], b[:, :, o_z:], zl, kr1, kr2, -kr2, kr1], axis=2),
        "gq": gq.reshape(n, 1, Q_LORA),
        "gkv": gkv.reshape(n, 1, KV_LORA),
        "wuq_t": jnp.swapaxes(_bf16(w_uq), 1, 2),
        "wuk": _bf16(w_uk),
        "wuv_t": jnp.swapaxes(_bf16(w_uv), 1, 2),
        "wo": _bf16(w_o),
        "g1": ln_g[0::2].reshape(n, 1, D_MODEL), "b1": ln_b[0::2].reshape(n, 1, D_MODEL),
        "g2": ln_g[1::2].reshape(n, 1, D_MODEL), "b2": ln_b[1::2].reshape(n, 1, D_MODEL),
        "win": _bf16(conv_w_in), "wout": _bf16(conv_w_out),
        "cw": jnp.pad(conv_w.astype(jnp.float32), ((0, 0), (0, SUBLANES - CONV_K), (0, 0))),
    }
    return [{name: v[j] for name, v in stacked.items()} for j in range(n)]


def _rope_tables(pos):
    inv_freq = ROPE_BASE ** (-jnp.arange(0, QK_ROPE, 2, dtype=jnp.float32) / QK_ROPE)
    ang = pos[..., None] * inv_freq
    return jnp.swapaxes(jnp.cos(ang), 1, 2), jnp.swapaxes(jnp.sin(ang), 1, 2)


def kernel(x, positions, meta_tokens, ln_g, ln_b, mla_w_in, mla_q_norm_g, mla_w_uq, mla_kv_norm_g,
           mla_w_uk, mla_w_uv, mla_w_o, conv_w_in, conv_w, conv_w_out):
    bsz, seq, _ = x.shape
    assert seq % CONV_TILE == 0 and CONV_TILE % TOKEN_TILE == 0 and TOKEN_TILE % CHUNK == 0
    assert seq % PROJ_TILE == 0 and PROJ_TILE % TOKEN_TILE == 0
    assert DEPTH % 2 == 0
    f32 = jnp.float32

    hf = x.astype(f32)
    hm = jnp.pad(meta_tokens.astype(f32), ((0, META_PAD - N_META), (0, 0)))[None]

    frame_tabs = _rope_tables((positions + N_META).astype(f32))
    meta_pos = jnp.pad(jnp.arange(N_META, dtype=f32), (0, META_PAD - N_META))[None]
    meta_tabs = _rope_tables(meta_pos)

    weights = _prep_weights(ln_g, ln_b, mla_w_in, mla_q_norm_g, mla_w_uq, mla_kv_norm_g,
                            mla_w_uk, mla_w_uv, mla_w_o, conv_w_in, conv_w, conv_w_out)
    for w in weights:
        frames, meta = _mla_proj(hf, hm, frame_tabs, meta_tabs, w)
        ot_f, ot_m = _attention(frames, meta)
        hf, hm = _out_conv(ot_f, frames[3], hf, ot_m, meta[3], hm, w)
    return hf
```

```python
import jax
import jax.numpy as jnp
from jax import lax
from jax.experimental import pallas as pl
from jax.experimental.pallas import tpu as pltpu

D_MODEL = 1024
DEPTH = 4
CHUNK = 64
CHUNK_SHIFT = CHUNK.bit_length() - 1
N_META = 16
MLA_HEADS = 16
QK_NOPE = 64
QK_ROPE = 32
HALF_ROPE = QK_ROPE // 2
V_HEAD = 64
Q_LORA = 384
KV_LORA = 256
MLA_WIDTH = MLA_HEADS * V_HEAD
ROPE_BASE = 10000.0
CONV_WIDTH = D_MODEL
CONV_K = 3
DN_ALPHA = (2 * DEPTH) ** 0.25
LN_EPS = 1e-5
RMS_EPS = 1e-6
NEG_INF = -1e30
LOG2_E = 1.4426950408889634

QK_HEAD = QK_NOPE + QK_ROPE
QK_WIDTH = MLA_HEADS * QK_HEAD
HEAD_SLOT = 128
ODD_SHIFT = HEAD_SLOT - QK_NOPE
BF16_ROWS = 16
V_SLOT = V_HEAD + BF16_ROWS
V_WIDTH = MLA_HEADS * V_SLOT
META_PAD = 128
TOKEN_TILE = 256
CONV_TILE = 512
PROJ_TILE = 512
SUBLANES = 8
VMEM_LIMIT = 56 * 1024 * 1024

_OFF_CKV = Q_LORA
_OFF_Z = Q_LORA + KV_LORA
_OFF_ROPE = _OFF_Z + MLA_WIDTH
_W1_COLS = _OFF_ROPE + HEAD_SLOT

_NT = (((1,), (1,)), ((), ()))


def _bf16(x):
    return x.astype(jnp.bfloat16)


def _dot(a, b):
    return jnp.dot(a, b, preferred_element_type=jnp.float32)


def _dot_nt(a, b):
    return lax.dot_general(a, b, _NT, preferred_element_type=jnp.float32)


def _rms(x, g):
    y = x * lax.rsqrt(jnp.mean(jnp.square(x), axis=-1, keepdims=True) + RMS_EPS)
    return y * g


def _layer_norm(x, g, b):
    mu = jnp.mean(x, axis=-1, keepdims=True)
    xc = x - mu
    var = jnp.mean(jnp.square(xc), axis=-1, keepdims=True)
    return xc * lax.rsqrt(var + LN_EPS) * g + b


def _silu(z):
    return z * (1.0 / (1.0 + jnp.exp(-z)))


def _head_rows(hd, width):
    return slice(hd * width, (hd + 1) * width)


def _first_step():
    return (pl.program_id(0) == 0) & (pl.program_id(1) == 0)


def _const_spec(shape, pipeline_mode=None):
    return pl.BlockSpec(shape, lambda b, i: (0,) * len(shape), pipeline_mode=pipeline_mode)


def _layer_spec(layer, shape, pipeline_mode=None):
    return pl.BlockSpec((None,) + shape, lambda b, i: (layer,) + (0,) * len(shape),
                        pipeline_mode=pipeline_mode)


def _mla_proj_tile(h_ref, cos_t_ref, sin_t_ref, w1_ref, gq_ref, gkv_ref, wuq_ref, wuk_ref, wuv_ref,
                   qt_ref, k_ref, vt_ref, gate_ref, causal_lanes, chunk):
    tm = h_ref.shape[1]
    chunks = [slice(c0, c0 + chunk) for c0 in range(0, tm, chunk)]

    def slot_table(t):
        rows = [jnp.zeros((QK_NOPE, chunk), t.dtype), t, t,
                jnp.zeros((HEAD_SLOT - QK_HEAD, chunk), t.dtype)]
        return jnp.concatenate(rows, axis=0).T

    ps = [_dot(_bf16(h_ref[0, rows, :]), w1_ref[...]) for rows in chunks]

    lora = []
    for rows, p in zip(chunks, ps):
        cqn = _bf16(_rms(p[:, :Q_LORA], gq_ref[...]))
        ckvn = _bf16(_rms(p[:, _OFF_CKV:_OFF_Z], gkv_ref[...]))
        gate_ref[0, rows, :] = _bf16(_silu(p[:, _OFF_Z:_OFF_ROPE]))
        cos_t = cos_t_ref[0, :, rows]
        sin_t = sin_t_ref[0, :, rows]
        lora.append((cqn, ckvn, cos_t, sin_t))

        ab = p[:, _OFF_ROPE:_W1_COLS]
        kr_even = ab * slot_table(cos_t) + pltpu.roll(ab, HEAD_SLOT - QK_ROPE, 1) * slot_table(sin_t)
        if causal_lanes:
            row = lax.broadcasted_iota(jnp.int32, kr_even.shape, 0)
            c = lax.broadcasted_iota(jnp.int32, kr_even.shape, 1) - QK_HEAD
            later_chunk = (c >= 0) & (c < chunk // CHUNK - 1) & (row >> CHUNK_SHIFT > c)
            kr_even = jnp.where(later_chunk, 1.0, kr_even)
        kr_odd = pltpu.roll(kr_even, ODD_SHIFT, 1)
        kn = _dot(ckvn, wuk_ref[...])
        low_half = lax.broadcasted_iota(jnp.int32, kr_even.shape, 1) < QK_NOPE
        for hd in range(0, MLA_HEADS, 2):
            pair = kn[:, hd * QK_NOPE:(hd + 2) * QK_NOPE]
            k_ref[0, hd, rows, :] = _bf16(jnp.where(low_half, pair, kr_even))
            k_ref[0, hd + 1, rows, :] = _bf16(jnp.where(low_half, kr_odd, pair))

    ones_row = (lax.broadcasted_iota(jnp.int32, (BF16_ROWS, chunk), 0) == 0).astype(jnp.bfloat16)
    for ci, (_, ckvn, _, _) in enumerate(lora):
        vt = _dot_nt(wuv_ref[...], ckvn)
        for hd in range(MLA_HEADS):
            vt_ref[0, ci, hd * V_SLOT:hd * V_SLOT + V_HEAD, :] = _bf16(vt[_head_rows(hd, V_HEAD)])
            vt_ref[0, ci, hd * V_SLOT + V_HEAD:(hd + 1) * V_SLOT, :] = ones_row

    scale = QK_HEAD ** -0.5 * LOG2_E
    for rows, (cqn, _, cos_t, sin_t) in zip(chunks, lora):
        qt = _dot_nt(wuq_ref[...], cqn) * scale
        for hd in range(MLA_HEADS):
            base = hd * QK_HEAD
            nope = _bf16(qt[base:base + QK_NOPE])
            x1 = qt[base + QK_NOPE:base + QK_NOPE + HALF_ROPE]
            x2 = qt[base + QK_NOPE + HALF_ROPE:base + QK_HEAD]
            rope = _bf16(jnp.concatenate([x1 * cos_t - x2 * sin_t, x1 * sin_t + x2 * cos_t], axis=0))
            if hd % 2 == 0:
                qt_ref[0, base:base + QK_NOPE, rows] = nope
                qt_ref[0, base + QK_NOPE:base + QK_HEAD, rows] = rope
            else:
                qt_ref[0, base:base + QK_ROPE, rows] = rope
                qt_ref[0, base + QK_ROPE:base + QK_HEAD, rows] = nope


def _mla_proj_kernel(h_ref, cos_ref, sin_ref, hm_ref, cosm_ref, sinm_ref,
                     w1_ref, gq_ref, gkv_ref, wuq_ref, wuk_ref, wuv_ref,
                     qt_ref, k_ref, vt_ref, gate_ref, qtm_ref, km_ref, vtm_ref, gatem_ref):
    weights = (w1_ref, gq_ref, gkv_ref, wuq_ref, wuk_ref, wuv_ref)

    @pl.when(_first_step())
    def _():
        _mla_proj_tile(hm_ref, cosm_ref, sinm_ref, *weights, qtm_ref, km_ref, vtm_ref, gatem_ref,
                       causal_lanes=False, chunk=META_PAD)

    _mla_proj_tile(h_ref, cos_ref, sin_ref, *weights, qt_ref, k_ref, vt_ref, gate_ref,
                   causal_lanes=True, chunk=TOKEN_TILE)


def _mla_proj(h, hm, tabs, meta_tabs, w, layer):
    bx, lx, _ = h.shape
    tm = PROJ_TILE
    per_step = tm // TOKEN_TILE
    tile = lambda b, i: (b, i, 0)
    tile_t = lambda b, i: (b, 0, i)

    def outputs(nb, n_tok, n_tiles, t):
        return [
            jax.ShapeDtypeStruct((nb, QK_WIDTH, n_tok), jnp.bfloat16),
            jax.ShapeDtypeStruct((nb, MLA_HEADS, n_tok, HEAD_SLOT), jnp.bfloat16),
            jax.ShapeDtypeStruct((nb, n_tiles, V_WIDTH, t), jnp.bfloat16),
            jax.ShapeDtypeStruct((nb, n_tok, MLA_WIDTH), jnp.bfloat16),
        ]

    res = pl.pallas_call(
        _mla_proj_kernel,
        grid=(bx, lx // tm),
        in_specs=[
            pl.BlockSpec((1, tm, D_MODEL), tile),
            pl.BlockSpec((1, HALF_ROPE, tm), tile_t),
            pl.BlockSpec((1, HALF_ROPE, tm), tile_t),
            _const_spec((1, META_PAD, D_MODEL)),
            _const_spec((1, HALF_ROPE, META_PAD)),
            _const_spec((1, HALF_ROPE, META_PAD)),
            _layer_spec(layer, (D_MODEL, _W1_COLS)),
            _layer_spec(layer, (1, Q_LORA)),
            _layer_spec(layer, (1, KV_LORA)),
            _layer_spec(layer, (QK_WIDTH, Q_LORA)),
            _layer_spec(layer, (KV_LORA, MLA_HEADS * QK_NOPE)),
            _layer_spec(layer, (MLA_WIDTH, KV_LORA)),
        ],
        out_specs=[
            pl.BlockSpec((1, QK_WIDTH, tm), tile_t),
            pl.BlockSpec((1, MLA_HEADS, tm, HEAD_SLOT), lambda b, i: (b, 0, i, 0)),
            pl.BlockSpec((1, per_step, V_WIDTH, TOKEN_TILE), lambda b, i: (b, i, 0, 0)),
            pl.BlockSpec((1, tm, MLA_WIDTH), tile),
            _const_spec((1, QK_WIDTH, META_PAD)),
            _const_spec((1, MLA_HEADS, META_PAD, HEAD_SLOT)),
            _const_spec((1, 1, V_WIDTH, META_PAD)),
            _const_spec((1, META_PAD, MLA_WIDTH)),
        ],
        out_shape=outputs(bx, lx, lx // TOKEN_TILE, TOKEN_TILE) + outputs(1, META_PAD, 1, META_PAD),
        compiler_params=pltpu.CompilerParams(
            dimension_semantics=("arbitrary", "arbitrary"), vmem_limit_bytes=VMEM_LIMIT),
        name="mla_proj",
    )(h, *tabs, hm, *meta_tabs, w["w1"], w["gq"], w["gkv"], w["wuq_t"], w["wuk"], w["wuv_t"])
    return res[:4], res[4:]


def _softmax_start(s):
    m = jnp.max(s, axis=0, keepdims=True)
    return m, jnp.exp2(s - m)


def _normalize(acc):
    return acc[:V_HEAD] / acc[V_HEAD:V_HEAD + 1]


def _q_operand(q_ref, hd, mask):
    q = q_ref[0, _head_rows(hd, QK_HEAD), :]
    zero = jnp.zeros((BF16_ROWS, q.shape[1]), q.dtype)
    spare = [zero if mask is None else mask, zero]
    parts = [q] + spare if hd % 2 == 0 else [q[:QK_ROPE]] + spare + [q[QK_ROPE:]]
    return jnp.concatenate(parts, axis=0)


def _attn_kernel(qt_ref, qn_ref, kb_ref, vb_ref, qtm_ref, km_ref, vm_ref, ot_ref, otm_ref,
                 kbuf_ref, vbuf_ref, s_ref, s0_ref, sm0_ref, m_ref, acc_ref):
    tk = vb_ref.shape[3]
    tq = qt_ref.shape[2]
    qi = pl.program_id(1)

    kbuf_ref[:, pl.ds(pl.multiple_of(qi * tk, tk), tk), :] = kb_ref[0]
    vbuf_ref[qi] = vb_ref[0, 0]

    row = lax.broadcasted_iota(jnp.int32, (BF16_ROWS, tq), 0)
    qry = lax.broadcasted_iota(jnp.int32, (BF16_ROWS, tq), 1)
    hide = (qry >> CHUNK_SHIFT == row) & (row < tk // CHUNK - 1)

    def mask_rows(is_diagonal):
        return _bf16(jnp.where(hide & is_diagonal, NEG_INF, 0.0))

    def scores(q_ref, hd, j, mask):
        row0 = pl.multiple_of(j * tk, tk)
        return _dot(kbuf_ref[hd, pl.ds(row0, tk), :], _q_operand(q_ref, hd, mask))

    def update(hd, s, j):
        m = m_ref[hd]
        m_new = jnp.maximum(m, jnp.max(s, axis=0, keepdims=True))
        alpha = jnp.exp2(m - m_new)
        p = jnp.exp2(s - m_new)
        m_ref[hd] = m_new
        acc_ref[hd] = alpha * acc_ref[hd] + _dot(vbuf_ref[j, _head_rows(hd, V_SLOT), :], _bf16(p))

    def meta_scores(q_ref):
        return [_dot(km_ref[0, hd], _q_operand(q_ref, hd, None))
                for hd in range(MLA_HEADS)]

    def meta_values(hd, p):
        no_weight = jnp.zeros((META_PAD - N_META, p.shape[1]), jnp.bfloat16)
        return _dot(vm_ref[0, 0, _head_rows(hd, V_SLOT), :],
                    jnp.concatenate([_bf16(p), no_weight], axis=0))

    @pl.when(_first_step())
    def _():
        sm = meta_scores(qtm_ref)
        for hd in range(MLA_HEADS):
            _, p = _softmax_start(sm[hd])
            otm_ref[0, _head_rows(hd, V_HEAD), :] = _bf16(_normalize(meta_values(hd, p)))

    @pl.when(qi == 0)
    def _():
        sm = meta_scores(qt_ref)
        diag_mask = mask_rows(True)
        for hd in range(MLA_HEADS):
            s_ref[0, hd] = scores(qt_ref, hd, 0, diag_mask)
            m, p = _softmax_start(sm[hd])
            m_ref[hd] = m
            acc_ref[hd] = meta_values(hd, p)

    @pl.when(qi > 0)
    def _():
        first_mask = mask_rows(qi == 1)
        for hd in range(MLA_HEADS):
            s_ref[1, hd] = scores(qt_ref, hd, 1, first_mask)
            sm = sm0_ref[hd]
            s = s0_ref[hd]
            m = jnp.maximum(jnp.max(sm, axis=0, keepdims=True), jnp.max(s, axis=0, keepdims=True))
            m_ref[hd] = m
            acc_ref[hd] = meta_values(hd, jnp.exp2(sm - m)) + _dot(
                vbuf_ref[0, _head_rows(hd, V_SLOT), :], _bf16(jnp.exp2(s - m)))

    def pair(j):
        for hd in range(MLA_HEADS):
            s_ref[0, hd] = scores(qt_ref, hd, j + 1, None)
            update(hd, s_ref[1, hd], j)
        last_mask = mask_rows(j + 2 == qi)
        for hd in range(MLA_HEADS):
            s_ref[1, hd] = scores(qt_ref, hd, j + 2, last_mask)
            update(hd, s_ref[0, hd], j + 1)

    def quad_step(jj, carry):
        pair(4 * jj + 1)
        pair(4 * jj + 3)
        return carry

    n_mid = jnp.maximum(qi - 1, 0)
    lax.fori_loop(0, n_mid // 4, quad_step, 0)

    @pl.when(n_mid % 4 >= 2)
    def _():
        pair(4 * (n_mid // 4) + 1)

    @pl.when((qi >= 2) & (qi % 2 == 0))
    def _():
        diag_mask = mask_rows(True)
        for hd in range(MLA_HEADS):
            s_ref[0, hd] = scores(qt_ref, hd, qi, diag_mask)
            update(hd, s_ref[1, hd], qi - 1)

    for hd in range(MLA_HEADS):
        keys = jnp.concatenate([kbuf_ref[hd, 0:tk, :], km_ref[0, hd]], axis=0)
        s_next = _dot(keys, _q_operand(qn_ref, hd, None))
        s0_ref[hd] = s_next[:tk]
        sm0_ref[hd] = s_next[tk:]
        update(hd, s_ref[qi % 2, hd], qi)
    for hd in range(MLA_HEADS):
        ot_ref[0, _head_rows(hd, V_HEAD), :] = _bf16(_normalize(acc_ref[hd]))


def _attention(frames, meta):
    qt, k, vt, _ = frames
    qt_m, k_m, vt_m, _ = meta
    bx, _, lx = qt.shape
    nk, tk = vt.shape[1], vt.shape[3]
    tq = tk
    nq = lx // tq
    return pl.pallas_call(
        _attn_kernel,
        grid=(bx, nq),
        in_specs=[
            pl.BlockSpec((1, QK_WIDTH, tq), lambda b, i: (b, 0, i)),
            pl.BlockSpec((1, QK_WIDTH, tq), lambda b, i: (b, 0, jnp.minimum(i + 1, nq - 1))),
            pl.BlockSpec((1, MLA_HEADS, tk, HEAD_SLOT), lambda b, i: (b, 0, i, 0)),
            pl.BlockSpec((1, 1, V_WIDTH, tk), lambda b, i: (b, i, 0, 0)),
            _const_spec((1, QK_WIDTH, META_PAD)),
            _const_spec((1, MLA_HEADS, N_META, HEAD_SLOT)),
            _const_spec((1, 1, V_WIDTH, META_PAD)),
        ],
        out_specs=[
            pl.BlockSpec((1, MLA_WIDTH, tq), lambda b, i: (b, 0, i)),
            _const_spec((1, MLA_WIDTH, META_PAD)),
        ],
        out_shape=[
            jax.ShapeDtypeStruct((bx, MLA_WIDTH, lx), jnp.bfloat16),
            jax.ShapeDtypeStruct((1, MLA_WIDTH, META_PAD), jnp.bfloat16),
        ],
        scratch_shapes=[
            pltpu.VMEM((MLA_HEADS, lx, HEAD_SLOT), jnp.bfloat16),
            pltpu.VMEM((nk, V_WIDTH, tk), jnp.bfloat16),
            pltpu.VMEM((2, MLA_HEADS, tk, tq), jnp.float32),
            pltpu.VMEM((MLA_HEADS, tk, tq), jnp.float32),
            pltpu.VMEM((MLA_HEADS, N_META, tq), jnp.float32),
            pltpu.VMEM((MLA_HEADS, 1, tq), jnp.float32),
            pltpu.VMEM((MLA_HEADS, V_SLOT, tq), jnp.float32),
        ],
        compiler_params=pltpu.CompilerParams(
            dimension_semantics=("arbitrary", "arbitrary"), vmem_limit_bytes=VMEM_LIMIT),
        name="attention",
    )(qt, qt, k, vt, qt_m, k_m, vt_m)


def _out_conv_tile(ot_ref, gate_ref, h_ref, out_ref, cu_ref,
                   wo_ref, g1_ref, b1_ref, win_ref, cw_ref, wout_ref, g2_ref, b2_ref, chunk):
    tm = h_ref.shape[1]
    w = CONV_WIDTH
    chunks = [slice(c0, c0 + chunk) for c0 in range(0, tm, chunk)]

    attn = []
    for rows in chunks:
        o = ot_ref[0, :, rows].astype(jnp.float32).T
        y = _bf16(o * gate_ref[0, rows, :].astype(jnp.float32))
        attn.append(_dot(y, wo_ref[...]))

    hs, ys = [], []
    for rows, a in zip(chunks, attn):
        h = _layer_norm(DN_ALPHA * h_ref[0, rows, :] + a, g1_ref[...], b1_ref[...])
        hb = _bf16(h)
        cu = _dot(hb, win_ref[:, w:2 * w]) * _dot(hb, win_ref[:, 2 * w:3 * w])
        c0 = rows.start
        cu_ref[SUBLANES + c0:SUBLANES + c0 + chunk, :] = cu
        conv = cu_ref[SUBLANES - 2 + c0:SUBLANES - 2 + c0 + chunk, :] * cw_ref[0:1, :]
        conv = conv + cu_ref[SUBLANES - 1 + c0:SUBLANES - 1 + c0 + chunk, :] * cw_ref[1:2, :]
        conv = conv + cu * cw_ref[2:3, :]
        hs.append(h)
        ys.append(_dot(hb, win_ref[:, 0:w]) * conv * _silu(_dot(hb, win_ref[:, 3 * w:4 * w])))

    for rows, h, y in zip(chunks, hs, ys):
        r = DN_ALPHA * h + _dot(_bf16(y), wout_ref[...])
        out_ref[0, rows, :] = _layer_norm(r, g2_ref[...], b2_ref[...])


def _out_conv_kernel(ot_ref, gate_ref, h_ref, otm_ref, gatem_ref, hm_ref,
                     wo_ref, g1_ref, b1_ref, win_ref, cw_ref, wout_ref, g2_ref, b2_ref,
                     out_ref, outm_ref, cu_ref, meta_tail_ref):
    tm = h_ref.shape[1]
    weights = (wo_ref, g1_ref, b1_ref, win_ref, cw_ref, wout_ref, g2_ref, b2_ref)

    @pl.when(_first_step())
    def _():
        cu_ref[0:SUBLANES, :] = jnp.zeros((SUBLANES, CONV_WIDTH), jnp.float32)
        _out_conv_tile(otm_ref, gatem_ref, hm_ref, outm_ref, cu_ref, *weights, chunk=META_PAD)
        meta_tail_ref[...] = cu_ref[N_META:N_META + SUBLANES, :]

    @pl.when(pl.program_id(1) == 0)
    def _():
        cu_ref[0:SUBLANES, :] = meta_tail_ref[...]

    _out_conv_tile(ot_ref, gate_ref, h_ref, out_ref, cu_ref, *weights, chunk=TOKEN_TILE)
    cu_ref[0:SUBLANES, :] = cu_ref[tm:tm + SUBLANES, :]


def _out_conv(ot, gate, h, ot_m, gate_m, hm, w, layer):
    bx, lx, _ = h.shape
    tm = CONV_TILE
    tile = lambda b_, i: (b_, i, 0)
    once = pl.Buffered(1)
    return pl.pallas_call(
        _out_conv_kernel,
        grid=(bx, lx // tm),
        in_specs=[
            pl.BlockSpec((1, MLA_WIDTH, tm), lambda b_, i: (b_, 0, i)),
            pl.BlockSpec((1, tm, MLA_WIDTH), tile),
            pl.BlockSpec((1, tm, D_MODEL), tile),
            _const_spec((1, MLA_WIDTH, META_PAD), once),
            _const_spec((1, META_PAD, MLA_WIDTH), once),
            _const_spec((1, META_PAD, D_MODEL), once),
            _layer_spec(layer, (MLA_WIDTH, D_MODEL), once),
            _layer_spec(layer, (1, D_MODEL), once),
            _layer_spec(layer, (1, D_MODEL), once),
            _layer_spec(layer, (D_MODEL, 4 * CONV_WIDTH), once),
            _layer_spec(layer, (SUBLANES, CONV_WIDTH), once),
            _layer_spec(layer, (CONV_WIDTH, D_MODEL), once),
            _layer_spec(layer, (1, D_MODEL), once),
            _layer_spec(layer, (1, D_MODEL), once),
        ],
        out_specs=[
            pl.BlockSpec((1, tm, D_MODEL), tile),
            _const_spec((1, META_PAD, D_MODEL)),
        ],
        out_shape=[
            jax.ShapeDtypeStruct(h.shape, jnp.float32),
            jax.ShapeDtypeStruct(hm.shape, jnp.float32),
        ],
        scratch_shapes=[
            pltpu.VMEM((tm + SUBLANES, CONV_WIDTH), jnp.float32),
            pltpu.VMEM((SUBLANES, CONV_WIDTH), jnp.float32),
        ],
        compiler_params=pltpu.CompilerParams(
            dimension_semantics=("arbitrary", "arbitrary"), vmem_limit_bytes=VMEM_LIMIT),
        name="out_conv",
    )(ot, gate, h, ot_m, gate_m, hm,
      w["wo"], w["g1"], w["b1"], w["win"], w["cw"], w["wout"], w["g2"], w["b2"])


def _prep_weights(ln_g, ln_b, w_in, gq, w_uq, gkv, w_uk, w_uv, w_o, conv_w_in, conv_w, conv_w_out):
    n = w_in.shape[0]
    o_kr = Q_LORA + KV_LORA
    o_z = o_kr + QK_ROPE
    b = _bf16(w_in)
    kr1 = b[:, :, o_kr:o_kr + HALF_ROPE]
    kr2 = b[:, :, o_kr + HALF_ROPE:o_z]
    zl = jnp.zeros((n, D_MODEL, QK_NOPE), jnp.bfloat16)
    return {
        "w1": jnp.concatenate([b[:, :, :o_kr], b[:, :, o_z:], zl, kr1, kr2, -kr2, kr1], axis=2),
        "gq": gq.reshape(n, 1, Q_LORA),
        "gkv": gkv.reshape(n, 1, KV_LORA),
        "wuq_t": _bf16(jnp.swapaxes(w_uq, 1, 2)),
        "wuk": _bf16(w_uk),
        "wuv_t": _bf16(jnp.swapaxes(w_uv, 1, 2)),
        "wo": _bf16(w_o),
        "g1": ln_g[0::2].reshape(n, 1, D_MODEL), "b1": ln_b[0::2].reshape(n, 1, D_MODEL),
        "g2": ln_g[1::2].reshape(n, 1, D_MODEL), "b2": ln_b[1::2].reshape(n, 1, D_MODEL),
        "win": _bf16(conv_w_in), "wout": _bf16(conv_w_out),
        "cw": jnp.pad(conv_w.astype(jnp.float32), ((0, 0), (0, SUBLANES - CONV_K), (0, 0))),
    }


def _rope_tables(pos):
    inv_freq = ROPE_BASE ** (-jnp.arange(0, QK_ROPE, 2, dtype=jnp.float32) / QK_ROPE)
    ang = pos[..., None] * inv_freq
    return jnp.swapaxes(jnp.cos(ang), 1, 2), jnp.swapaxes(jnp.sin(ang), 1, 2)


def kernel(x, positions, meta_tokens, ln_g, ln_b, mla_w_in, mla_q_norm_g, mla_w_uq, mla_kv_norm_g,
           mla_w_uk, mla_w_uv, mla_w_o, conv_w_in, conv_w, conv_w_out):
    bsz, seq, _ = x.shape
    assert seq % CONV_TILE == 0 and CONV_TILE % TOKEN_TILE == 0 and TOKEN_TILE % CHUNK == 0
    assert seq % PROJ_TILE == 0 and PROJ_TILE % TOKEN_TILE == 0
    assert DEPTH % 2 == 0
    f32 = jnp.float32

    hf = x.astype(f32)
    hm = jnp.pad(meta_tokens.astype(f32), ((0, META_PAD - N_META), (0, 0)))[None]

    frame_tabs = _rope_tables((positions + N_META).astype(f32))
    meta_pos = jnp.pad(jnp.arange(N_META, dtype=f32), (0, META_PAD - N_META))[None]
    meta_tabs = _rope_tables(meta_pos)

    w = _prep_weights(ln_g, ln_b, mla_w_in, mla_q_norm_g, mla_w_uq, mla_kv_norm_g,
                      mla_w_uk, mla_w_uv, mla_w_o, conv_w_in, conv_w, conv_w_out)
    for j in range(DEPTH // 2):
        frames, meta = _mla_proj(hf, hm, frame_tabs, meta_tabs, w, j)
        ot_f, ot_m = _attention(frames, meta)
        hf, hm = _out_conv(ot_f, frames[3], hf, ot_m, meta[3], hm, w, j)
    return hf
```

```python
import jax
import jax.numpy as jnp
from jax import lax
from jax.experimental import pallas as pl
from jax.experimental.pallas import tpu as pltpu

D_MODEL = 1024
DEPTH = 4
CHUNK = 64
CHUNK_SHIFT = CHUNK.bit_length() - 1
N_META = 16
MLA_HEADS = 16
QK_NOPE = 64
QK_ROPE = 32
HALF_ROPE = QK_ROPE // 2
V_HEAD = 64
Q_LORA = 384
KV_LORA = 256
MLA_WIDTH = MLA_HEADS * V_HEAD
ROPE_BASE = 10000.0
CONV_WIDTH = D_MODEL
CONV_K = 3
DN_ALPHA = (2 * DEPTH) ** 0.25
LN_EPS = 1e-5
RMS_EPS = 1e-6
NEG_INF = -1e30
LOG2_E = 1.4426950408889634

QK_HEAD = QK_NOPE + QK_ROPE
QK_WIDTH = MLA_HEADS * QK_HEAD
HEAD_SLOT = 128
ODD_SHIFT = HEAD_SLOT - QK_NOPE
BF16_ROWS = 16
V_SLOT = V_HEAD + BF16_ROWS
V_WIDTH = MLA_HEADS * V_SLOT
META_PAD = 128
TOKEN_TILE = 256
CONV_TILE = 512
PROJ_TILE = 512
SUBLANES = 8
VMEM_LIMIT = 56 * 1024 * 1024

_OFF_CKV = Q_LORA
_OFF_Z = Q_LORA + KV_LORA
_OFF_ROPE = _OFF_Z + MLA_WIDTH
_W1_COLS = _OFF_ROPE + HEAD_SLOT

_NT = (((1,), (1,)), ((), ()))


def _bf16(x):
    return x.astype(jnp.bfloat16)


def _dot(a, b):
    return jnp.dot(a, b, preferred_element_type=jnp.float32)


def _dot_nt(a, b):
    return lax.dot_general(a, b, _NT, preferred_element_type=jnp.float32)


def _rms(x, g):
    y = x * lax.rsqrt(jnp.mean(jnp.square(x), axis=-1, keepdims=True) + RMS_EPS)
    return y * g


def _layer_norm(x, g, b):
    mu = jnp.mean(x, axis=-1, keepdims=True)
    xc = x - mu
    var = jnp.mean(jnp.square(xc), axis=-1, keepdims=True)
    return xc * lax.rsqrt(var + LN_EPS) * g + b


def _silu(z):
    return z * (1.0 / (1.0 + jnp.exp(-z)))


def _head_rows(hd, width):
    return slice(hd * width, (hd + 1) * width)


def _first_step():
    return (pl.program_id(0) == 0) & (pl.program_id(1) == 0)


def _const_spec(shape, pipeline_mode=None):
    return pl.BlockSpec(shape, lambda b, i: (0,) * len(shape), pipeline_mode=pipeline_mode)


def _mla_proj_tile(h_ref, cos_t_ref, sin_t_ref, w1_ref, gq_ref, gkv_ref, wuq_ref, wuk_ref, wuv_ref,
                   qt_ref, k_ref, vt_ref, gate_ref, causal_lanes, chunk):
    tm = h_ref.shape[1]
    chunks = [slice(c0, c0 + chunk) for c0 in range(0, tm, chunk)]

    def slot_table(t):
        rows = [jnp.zeros((QK_NOPE, chunk), t.dtype), t, t,
                jnp.zeros((HEAD_SLOT - QK_HEAD, chunk), t.dtype)]
        return jnp.concatenate(rows, axis=0).T

    ps = [_dot(_bf16(h_ref[0, rows, :]), w1_ref[...]) for rows in chunks]

    lora = []
    for rows, p in zip(chunks, ps):
        cqn = _bf16(_rms(p[:, :Q_LORA], gq_ref[...]))
        ckvn = _bf16(_rms(p[:, _OFF_CKV:_OFF_Z], gkv_ref[...]))
        gate_ref[0, rows, :] = _bf16(_silu(p[:, _OFF_Z:_OFF_ROPE]))
        cos_t = cos_t_ref[0, :, rows]
        sin_t = sin_t_ref[0, :, rows]
        lora.append((cqn, ckvn, cos_t, sin_t))

        ab = p[:, _OFF_ROPE:_W1_COLS]
        kr_even = ab * slot_table(cos_t) + pltpu.roll(ab, HEAD_SLOT - QK_ROPE, 1) * slot_table(sin_t)
        if causal_lanes:
            row = lax.broadcasted_iota(jnp.int32, kr_even.shape, 0)
            c = lax.broadcasted_iota(jnp.int32, kr_even.shape, 1) - QK_HEAD
            later_chunk = (c >= 0) & (c < chunk // CHUNK - 1) & (row >> CHUNK_SHIFT > c)
            kr_even = jnp.where(later_chunk, 1.0, kr_even)
        kr_odd = pltpu.roll(kr_even, ODD_SHIFT, 1)
        kn = _dot(ckvn, wuk_ref[...])
        low_half = lax.broadcasted_iota(jnp.int32, kr_even.shape, 1) < QK_NOPE
        for hd in range(0, MLA_HEADS, 2):
            pair = kn[:, hd * QK_NOPE:(hd + 2) * QK_NOPE]
            k_ref[0, hd, rows, :] = _bf16(jnp.where(low_half, pair, kr_even))
            k_ref[0, hd + 1, rows, :] = _bf16(jnp.where(low_half, kr_odd, pair))

    ones_row = (lax.broadcasted_iota(jnp.int32, (BF16_ROWS, chunk), 0) == 0).astype(jnp.bfloat16)
    for ci, (_, ckvn, _, _) in enumerate(lora):
        vt = _dot_nt(wuv_ref[...], ckvn)
        for hd in range(MLA_HEADS):
            vt_ref[0, ci, hd * V_SLOT:hd * V_SLOT + V_HEAD, :] = _bf16(vt[_head_rows(hd, V_HEAD)])
            vt_ref[0, ci, hd * V_SLOT + V_HEAD:(hd + 1) * V_SLOT, :] = ones_row

    scale = QK_HEAD ** -0.5 * LOG2_E
    for rows, (cqn, _, cos_t, sin_t) in zip(chunks, lora):
        qt = _dot_nt(wuq_ref[...], cqn) * scale
        for hd in range(MLA_HEADS):
            base = hd * QK_HEAD
            nope = _bf16(qt[base:base + QK_NOPE])
            x1 = qt[base + QK_NOPE:base + QK_NOPE + HALF_ROPE]
            x2 = qt[base + QK_NOPE + HALF_ROPE:base + QK_HEAD]
            rope = _bf16(jnp.concatenate([x1 * cos_t - x2 * sin_t, x1 * sin_t + x2 * cos_t], axis=0))
            if hd % 2 == 0:
                qt_ref[0, base:base + QK_NOPE, rows] = nope
                qt_ref[0, base + QK_NOPE:base + QK_HEAD, rows] = rope
            else:
                qt_ref[0, base:base + QK_ROPE, rows] = rope
                qt_ref[0, base + QK_ROPE:base + QK_HEAD, rows] = nope


def _mla_proj_kernel(h_ref, cos_ref, sin_ref, hm_ref, cosm_ref, sinm_ref,
                     w1_ref, gq_ref, gkv_ref, wuq_ref, wuk_ref, wuv_ref,
                     qt_ref, k_ref, vt_ref, gate_ref, qtm_ref, km_ref, vtm_ref, gatem_ref):
    weights = (w1_ref, gq_ref, gkv_ref, wuq_ref, wuk_ref, wuv_ref)

    @pl.when(_first_step())
    def _():
        _mla_proj_tile(hm_ref, cosm_ref, sinm_ref, *weights, qtm_ref, km_ref, vtm_ref, gatem_ref,
                       causal_lanes=False, chunk=META_PAD)

    _mla_proj_tile(h_ref, cos_ref, sin_ref, *weights, qt_ref, k_ref, vt_ref, gate_ref,
                   causal_lanes=True, chunk=TOKEN_TILE)


def _mla_proj(h, hm, tabs, meta_tabs, w):
    bx, lx, _ = h.shape
    tm = PROJ_TILE
    per_step = tm // TOKEN_TILE
    tile = lambda b, i: (b, i, 0)
    tile_t = lambda b, i: (b, 0, i)

    def outputs(nb, n_tok, n_tiles, t):
        return [
            jax.ShapeDtypeStruct((nb, QK_WIDTH, n_tok), jnp.bfloat16),
            jax.ShapeDtypeStruct((nb, MLA_HEADS, n_tok, HEAD_SLOT), jnp.bfloat16),
            jax.ShapeDtypeStruct((nb, n_tiles, V_WIDTH, t), jnp.bfloat16),
            jax.ShapeDtypeStruct((nb, n_tok, MLA_WIDTH), jnp.bfloat16),
        ]

    res = pl.pallas_call(
        _mla_proj_kernel,
        grid=(bx, lx // tm),
        in_specs=[
            pl.BlockSpec((1, tm, D_MODEL), tile),
            pl.BlockSpec((1, HALF_ROPE, tm), tile_t),
            pl.BlockSpec((1, HALF_ROPE, tm), tile_t),
            _const_spec((1, META_PAD, D_MODEL)),
            _const_spec((1, HALF_ROPE, META_PAD)),
            _const_spec((1, HALF_ROPE, META_PAD)),
            _const_spec((D_MODEL, _W1_COLS)),
            _const_spec((1, Q_LORA)),
            _const_spec((1, KV_LORA)),
            _const_spec((QK_WIDTH, Q_LORA)),
            _const_spec((KV_LORA, MLA_HEADS * QK_NOPE)),
            _const_spec((MLA_WIDTH, KV_LORA)),
        ],
        out_specs=[
            pl.BlockSpec((1, QK_WIDTH, tm), tile_t),
            pl.BlockSpec((1, MLA_HEADS, tm, HEAD_SLOT), lambda b, i: (b, 0, i, 0)),
            pl.BlockSpec((1, per_step, V_WIDTH, TOKEN_TILE), lambda b, i: (b, i, 0, 0)),
            pl.BlockSpec((1, tm, MLA_WIDTH), tile),
            _const_spec((1, QK_WIDTH, META_PAD)),
            _const_spec((1, MLA_HEADS, META_PAD, HEAD_SLOT)),
            _const_spec((1, 1, V_WIDTH, META_PAD)),
            _const_spec((1, META_PAD, MLA_WIDTH)),
        ],
        out_shape=outputs(bx, lx, lx // TOKEN_TILE, TOKEN_TILE) + outputs(1, META_PAD, 1, META_PAD),
        compiler_params=pltpu.CompilerParams(
            dimension_semantics=("arbitrary", "arbitrary"), vmem_limit_bytes=VMEM_LIMIT),
        name="mla_proj",
    )(h, *tabs, hm, *meta_tabs, w["w1"], w["gq"], w["gkv"], w["wuq_t"], w["wuk"], w["wuv_t"])
    return res[:4], res[4:]


def _softmax_start(s):
    m = jnp.max(s, axis=0, keepdims=True)
    return m, jnp.exp2(s - m)


def _normalize(acc):
    return acc[:V_HEAD] / acc[V_HEAD:V_HEAD + 1]


def _q_operand(q_ref, hd, mask):
    q = q_ref[0, _head_rows(hd, QK_HEAD), :]
    zero = jnp.zeros((BF16_ROWS, q.shape[1]), q.dtype)
    spare = [zero if mask is None else mask, zero]
    parts = [q] + spare if hd % 2 == 0 else [q[:QK_ROPE]] + spare + [q[QK_ROPE:]]
    return jnp.concatenate(parts, axis=0)


def _attn_kernel(qt_ref, qn_ref, kb_ref, vb_ref, qtm_ref, km_ref, vm_ref, ot_ref, otm_ref,
                 kbuf_ref, vbuf_ref, s_ref, s0_ref, sm0_ref, m_ref, acc_ref):
    tk = vb_ref.shape[3]
    tq = qt_ref.shape[2]
    qi = pl.program_id(1)

    kbuf_ref[:, pl.ds(pl.multiple_of(qi * tk, tk), tk), :] = kb_ref[0]
    vbuf_ref[qi] = vb_ref[0, 0]

    row = lax.broadcasted_iota(jnp.int32, (BF16_ROWS, tq), 0)
    qry = lax.broadcasted_iota(jnp.int32, (BF16_ROWS, tq), 1)
    hide = (qry >> CHUNK_SHIFT == row) & (row < tk // CHUNK - 1)

    def mask_rows(is_diagonal):
        return _bf16(jnp.where(hide & is_diagonal, NEG_INF, 0.0))

    def scores(q_ref, hd, j, mask):
        row0 = pl.multiple_of(j * tk, tk)
        return _dot(kbuf_ref[hd, pl.ds(row0, tk), :], _q_operand(q_ref, hd, mask))

    def update(hd, s, j):
        m = m_ref[hd]
        m_new = jnp.maximum(m, jnp.max(s, axis=0, keepdims=True))
        alpha = jnp.exp2(m - m_new)
        p = jnp.exp2(s - m_new)
        m_ref[hd] = m_new
        acc_ref[hd] = alpha * acc_ref[hd] + _dot(vbuf_ref[j, _head_rows(hd, V_SLOT), :], _bf16(p))

    def meta_scores(q_ref):
        return [_dot(km_ref[0, hd], _q_operand(q_ref, hd, None))
                for hd in range(MLA_HEADS)]

    def meta_values(hd, p):
        no_weight = jnp.zeros((META_PAD - N_META, p.shape[1]), jnp.bfloat16)
        return _dot(vm_ref[0, 0, _head_rows(hd, V_SLOT), :],
                    jnp.concatenate([_bf16(p), no_weight], axis=0))

    @pl.when(_first_step())
    def _():
        sm = meta_scores(qtm_ref)
        for hd in range(MLA_HEADS):
            _, p = _softmax_start(sm[hd])
            otm_ref[0, _head_rows(hd, V_HEAD), :] = _bf16(_normalize(meta_values(hd, p)))

    @pl.when(qi == 0)
    def _():
        sm = meta_scores(qt_ref)
        diag_mask = mask_rows(True)
        for hd in range(MLA_HEADS):
            s_ref[0, hd] = scores(qt_ref, hd, 0, diag_mask)
            m, p = _softmax_start(sm[hd])
            m_ref[hd] = m
            acc_ref[hd] = meta_values(hd, p)

    @pl.when(qi > 0)
    def _():
        first_mask = mask_rows(qi == 1)
        for hd in range(MLA_HEADS):
            s_ref[1, hd] = scores(qt_ref, hd, 1, first_mask)
            sm = sm0_ref[hd]
            s = s0_ref[hd]
            m = jnp.maximum(jnp.max(sm, axis=0, keepdims=True), jnp.max(s, axis=0, keepdims=True))
            m_ref[hd] = m
            acc_ref[hd] = meta_values(hd, jnp.exp2(sm - m)) + _dot(
                vbuf_ref[0, _head_rows(hd, V_SLOT), :], _bf16(jnp.exp2(s - m)))

    def pair(j):
        for hd in range(MLA_HEADS):
            s_ref[0, hd] = scores(qt_ref, hd, j + 1, None)
            update(hd, s_ref[1, hd], j)
        last_mask = mask_rows(j + 2 == qi)
        for hd in range(MLA_HEADS):
            s_ref[1, hd] = scores(qt_ref, hd, j + 2, last_mask)
            update(hd, s_ref[0, hd], j + 1)

    def quad_step(jj, carry):
        pair(4 * jj + 1)
        pair(4 * jj + 3)
        return carry

    n_mid = jnp.maximum(qi - 1, 0)
    lax.fori_loop(0, n_mid // 4, quad_step, 0)

    @pl.when(n_mid % 4 >= 2)
    def _():
        pair(4 * (n_mid // 4) + 1)

    @pl.when((qi >= 2) & (qi % 2 == 0))
    def _():
        diag_mask = mask_rows(True)
        for hd in range(MLA_HEADS):
            s_ref[0, hd] = scores(qt_ref, hd, qi, diag_mask)
            update(hd, s_ref[1, hd], qi - 1)

    for hd in range(MLA_HEADS):
        keys = jnp.concatenate([kbuf_ref[hd, 0:tk, :], km_ref[0, hd]], axis=0)
        s_next = _dot(keys, _q_operand(qn_ref, hd, None))
        s0_ref[hd] = s_next[:tk]
        sm0_ref[hd] = s_next[tk:]
        update(hd, s_ref[qi % 2, hd], qi)
    for hd in range(MLA_HEADS):
        ot_ref[0, _head_rows(hd, V_HEAD), :] = _bf16(_normalize(acc_ref[hd]))


def _attention(frames, meta):
    qt, k, vt, _ = frames
    qt_m, k_m, vt_m, _ = meta
    bx, _, lx = qt.shape
    nk, tk = vt.shape[1], vt.shape[3]
    tq = tk
    nq = lx // tq
    return pl.pallas_call(
        _attn_kernel,
        grid=(bx, nq),
        in_specs=[
            pl.BlockSpec((1, QK_WIDTH, tq), lambda b, i: (b, 0, i)),
            pl.BlockSpec((1, QK_WIDTH, tq), lambda b, i: (b, 0, jnp.minimum(i + 1, nq - 1))),
            pl.BlockSpec((1, MLA_HEADS, tk, HEAD_SLOT), lambda b, i: (b, 0, i, 0)),
            pl.BlockSpec((1, 1, V_WIDTH, tk), lambda b, i: (b, i, 0, 0)),
            _const_spec((1, QK_WIDTH, META_PAD)),
            _const_spec((1, MLA_HEADS, N_META, HEAD_SLOT)),
            _const_spec((1, 1, V_WIDTH, META_PAD)),
        ],
        out_specs=[
            pl.BlockSpec((1, MLA_WIDTH, tq), lambda b, i: (b, 0, i)),
            _const_spec((1, MLA_WIDTH, META_PAD)),
        ],
        out_shape=[
            jax.ShapeDtypeStruct((bx, MLA_WIDTH, lx), jnp.bfloat16),
            jax.ShapeDtypeStruct((1, MLA_WIDTH, META_PAD), jnp.bfloat16),
        ],
        scratch_shapes=[
            pltpu.VMEM((MLA_HEADS, lx, HEAD_SLOT), jnp.bfloat16),
            pltpu.VMEM((nk, V_WIDTH, tk), jnp.bfloat16),
            pltpu.VMEM((2, MLA_HEADS, tk, tq), jnp.float32),
            pltpu.VMEM((MLA_HEADS, tk, tq), jnp.float32),
            pltpu.VMEM((MLA_HEADS, N_META, tq), jnp.float32),
            pltpu.VMEM((MLA_HEADS, 1, tq), jnp.float32),
            pltpu.VMEM((MLA_HEADS, V_SLOT, tq), jnp.float32),
        ],
        compiler_params=pltpu.CompilerParams(
            dimension_semantics=("arbitrary", "arbitrary"), vmem_limit_bytes=VMEM_LIMIT),
        name="attention",
    )(qt, qt, k, vt, qt_m, k_m, vt_m)


def _out_conv_tile(ot_ref, gate_ref, h_ref, out_ref, cu_ref,
                   wo_ref, g1_ref, b1_ref, win_ref, cw_ref, wout_ref, g2_ref, b2_ref, chunk):
    tm = h_ref.shape[1]
    w = CONV_WIDTH
    chunks = [slice(c0, c0 + chunk) for c0 in range(0, tm, chunk)]

    attn = []
    for rows in chunks:
        o = ot_ref[0, :, rows].astype(jnp.float32).T
        y = _bf16(o * gate_ref[0, rows, :].astype(jnp.float32))
        attn.append(_dot(y, wo_ref[...]))

    hs, ys = [], []
    for rows, a in zip(chunks, attn):
        h = _layer_norm(DN_ALPHA * h_ref[0, rows, :] + a, g1_ref[...], b1_ref[...])
        hb = _bf16(h)
        cu = _dot(hb, win_ref[:, w:2 * w]) * _dot(hb, win_ref[:, 2 * w:3 * w])
        c0 = rows.start
        cu_ref[SUBLANES + c0:SUBLANES + c0 + chunk, :] = cu
        conv = cu_ref[SUBLANES - 2 + c0:SUBLANES - 2 + c0 + chunk, :] * cw_ref[0:1, :]
        conv = conv + cu_ref[SUBLANES - 1 + c0:SUBLANES - 1 + c0 + chunk, :] * cw_ref[1:2, :]
        conv = conv + cu * cw_ref[2:3, :]
        hs.append(h)
        ys.append(_dot(hb, win_ref[:, 0:w]) * conv * _silu(_dot(hb, win_ref[:, 3 * w:4 * w])))

    for rows, h, y in zip(chunks, hs, ys):
        r = DN_ALPHA * h + _dot(_bf16(y), wout_ref[...])
        out_ref[0, rows, :] = _layer_norm(r, g2_ref[...], b2_ref[...])


def _out_conv_kernel(ot_ref, gate_ref, h_ref, otm_ref, gatem_ref, hm_ref,
                     wo_ref, g1_ref, b1_ref, win_ref, cw_ref, wout_ref, g2_ref, b2_ref,
                     out_ref, outm_ref, cu_ref, meta_tail_ref):
    tm = h_ref.shape[1]
    weights = (wo_ref, g1_ref, b1_ref, win_ref, cw_ref, wout_ref, g2_ref, b2_ref)

    @pl.when(_first_step())
    def _():
        cu_ref[0:SUBLANES, :] = jnp.zeros((SUBLANES, CONV_WIDTH), jnp.float32)
        _out_conv_tile(otm_ref, gatem_ref, hm_ref, outm_ref, cu_ref, *weights, chunk=META_PAD)
        meta_tail_ref[...] = cu_ref[N_META:N_META + SUBLANES, :]

    @pl.when(pl.program_id(1) == 0)
    def _():
        cu_ref[0:SUBLANES, :] = meta_tail_ref[...]

    _out_conv_tile(ot_ref, gate_ref, h_ref, out_ref, cu_ref, *weights, chunk=TOKEN_TILE)
    cu_ref[0:SUBLANES, :] = cu_ref[tm:tm + SUBLANES, :]


def _out_conv(ot, gate, h, ot_m, gate_m, hm, w):
    bx, lx, _ = h.shape
    tm = CONV_TILE
    tile = lambda b_, i: (b_, i, 0)
    once = pl.Buffered(1)
    return pl.pallas_call(
        _out_conv_kernel,
        grid=(bx, lx // tm),
        in_specs=[
            pl.BlockSpec((1, MLA_WIDTH, tm), lambda b_, i: (b_, 0, i)),
            pl.BlockSpec((1, tm, MLA_WIDTH), tile),
            pl.BlockSpec((1, tm, D_MODEL), tile),
            _const_spec((1, MLA_WIDTH, META_PAD), once),
            _const_spec((1, META_PAD, MLA_WIDTH), once),
            _const_spec((1, META_PAD, D_MODEL), once),
            _const_spec((MLA_WIDTH, D_MODEL), once),
            _const_spec((1, D_MODEL), once),
            _const_spec((1, D_MODEL), once),
            _const_spec((D_MODEL, 4 * CONV_WIDTH), once),
            _const_spec((SUBLANES, CONV_WIDTH), once),
            _const_spec((CONV_WIDTH, D_MODEL), once),
            _const_spec((1, D_MODEL), once),
            _const_spec((1, D_MODEL), once),
        ],
        out_specs=[
            pl.BlockSpec((1, tm, D_MODEL), tile),
            _const_spec((1, META_PAD, D_MODEL)),
        ],
        out_shape=[
            jax.ShapeDtypeStruct(h.shape, jnp.float32),
            jax.ShapeDtypeStruct(hm.shape, jnp.float32),
        ],
        scratch_shapes=[
            pltpu.VMEM((tm + SUBLANES, CONV_WIDTH), jnp.float32),
            pltpu.VMEM((SUBLANES, CONV_WIDTH), jnp.float32),
        ],
        compiler_params=pltpu.CompilerParams(
            dimension_semantics=("arbitrary", "arbitrary"), vmem_limit_bytes=VMEM_LIMIT),
        name="out_conv",
    )(ot, gate, h, ot_m, gate_m, hm,
      w["wo"], w["g1"], w["b1"], w["win"], w["cw"], w["wout"], w["g2"], w["b2"])


def _prep_weights(j, ln_g, ln_b, w_in, gq, w_uq, gkv, w_uk, w_uv, w_o, conv_w_in, conv_w, conv_w_out):
    o_kr = Q_LORA + KV_LORA
    o_z = o_kr + QK_ROPE
    w = w_in[j]
    kr1 = _bf16(w[:, o_kr:o_kr + HALF_ROPE])
    kr2 = _bf16(w[:, o_kr + HALF_ROPE:o_z])
    zl = jnp.zeros((D_MODEL, QK_NOPE), jnp.bfloat16)
    return {
        "w1": jnp.concatenate([_bf16(w[:, :o_kr]), _bf16(w[:, o_z:]), zl, kr1, kr2, -kr2, kr1], axis=1),
        "gq": gq[j].reshape(1, Q_LORA),
        "gkv": gkv[j].reshape(1, KV_LORA),
        "wuq_t": _bf16(w_uq[j].T),
        "wuk": _bf16(w_uk[j]),
        "wuv_t": _bf16(w_uv[j].T),
        "wo": _bf16(w_o[j]),
        "g1": ln_g[2 * j].reshape(1, D_MODEL), "b1": ln_b[2 * j].reshape(1, D_MODEL),
        "g2": ln_g[2 * j + 1].reshape(1, D_MODEL), "b2": ln_b[2 * j + 1].reshape(1, D_MODEL),
        "win": _bf16(conv_w_in[j]), "wout": _bf16(conv_w_out[j]),
        "cw": jnp.pad(conv_w[j].astype(jnp.float32), ((0, SUBLANES - CONV_K), (0, 0))),
    }


def _rope_tables(pos):
    inv_freq = ROPE_BASE ** (-jnp.arange(0, QK_ROPE, 2, dtype=jnp.float32) / QK_ROPE)
    ang = pos[..., None] * inv_freq
    return jnp.swapaxes(jnp.cos(ang), 1, 2), jnp.swapaxes(jnp.sin(ang), 1, 2)


def kernel(x, positions, meta_tokens, ln_g, ln_b, mla_w_in, mla_q_norm_g, mla_w_uq, mla_kv_norm_g,
           mla_w_uk, mla_w_uv, mla_w_o, conv_w_in, conv_w, conv_w_out):
    bsz, seq, _ = x.shape
    assert seq % CONV_TILE == 0 and CONV_TILE % TOKEN_TILE == 0 and TOKEN_TILE % CHUNK == 0
    assert seq % PROJ_TILE == 0 and PROJ_TILE % TOKEN_TILE == 0
    assert DEPTH % 2 == 0
    f32 = jnp.float32

    hf = x.astype(f32)
    hm = jnp.pad(meta_tokens.astype(f32), ((0, META_PAD - N_META), (0, 0)))[None]

    frame_tabs = _rope_tables((positions + N_META).astype(f32))
    meta_pos = jnp.pad(jnp.arange(N_META, dtype=f32), (0, META_PAD - N_META))[None]
    meta_tabs = _rope_tables(meta_pos)

    for j in range(DEPTH // 2):
        w = _prep_weights(j, ln_g, ln_b, mla_w_in, mla_q_norm_g, mla_w_uq, mla_kv_norm_g,
                          mla_w_uk, mla_w_uv, mla_w_o, conv_w_in, conv_w, conv_w_out)
        frames, meta = _mla_proj(hf, hm, frame_tabs, meta_tabs, w)
        ot_f, ot_m = _attention(frames, meta)
        hf, hm = _out_conv(ot_f, frames[3], hf, ot_m, meta[3], hm, w)
    return hf
```

```python
import jax
import jax.numpy as jnp
from jax import lax
from jax.experimental import pallas as pl
from jax.experimental.pallas import tpu as pltpu

D_MODEL = 1024
DEPTH = 4
CHUNK = 64
CHUNK_SHIFT = CHUNK.bit_length() - 1
N_META = 16
MLA_HEADS = 16
QK_NOPE = 64
QK_ROPE = 32
HALF_ROPE = QK_ROPE // 2
V_HEAD = 64
Q_LORA = 384
KV_LORA = 256
MLA_WIDTH = MLA_HEADS * V_HEAD
ROPE_BASE = 10000.0
CONV_WIDTH = D_MODEL
CONV_K = 3
DN_ALPHA = (2 * DEPTH) ** 0.25
LN_EPS = 1e-5
RMS_EPS = 1e-6
NEG_INF = -1e30
LOG2_E = 1.4426950408889634

QK_HEAD = QK_NOPE + QK_ROPE
QK_WIDTH = MLA_HEADS * QK_HEAD
HEAD_SLOT = 128
ODD_SHIFT = HEAD_SLOT - QK_NOPE
BF16_ROWS = 16
V_SLOT = V_HEAD + BF16_ROWS
V_WIDTH = MLA_HEADS * V_SLOT
META_PAD = 128
TOKEN_TILE = 256
CONV_TILE = 1024
PROJ_TILE = 1024
SUBLANES = 8
VMEM_LIMIT = 56 * 1024 * 1024

_OFF_CKV = Q_LORA
_OFF_Z = Q_LORA + KV_LORA
_OFF_ROPE = _OFF_Z + MLA_WIDTH
_W1_COLS = _OFF_ROPE + HEAD_SLOT

_NT = (((1,), (1,)), ((), ()))


def _bf16(x):
    return x.astype(jnp.bfloat16)


def _dot(a, b):
    return jnp.dot(a, b, preferred_element_type=jnp.float32)


def _dot_nt(a, b):
    return lax.dot_general(a, b, _NT, preferred_element_type=jnp.float32)


def _rms(x, g):
    y = x * lax.rsqrt(jnp.mean(jnp.square(x), axis=-1, keepdims=True) + RMS_EPS)
    return y * g


def _layer_norm(x, g, b):
    mu = jnp.mean(x, axis=-1, keepdims=True)
    xc = x - mu
    var = jnp.mean(jnp.square(xc), axis=-1, keepdims=True)
    return xc * lax.rsqrt(var + LN_EPS) * g + b


def _silu(z):
    return z * (1.0 / (1.0 + jnp.exp(-z)))


def _head_rows(hd, width):
    return slice(hd * width, (hd + 1) * width)


def _first_step():
    return (pl.program_id(0) == 0) & (pl.program_id(1) == 0)


def _const_spec(shape, pipeline_mode=None):
    return pl.BlockSpec(shape, lambda b, i: (0,) * len(shape), pipeline_mode=pipeline_mode)


def _mla_proj_tile(h_ref, cos_t_ref, sin_t_ref, w1_ref, gq_ref, gkv_ref, wuq_ref, wuk_ref, wuv_ref,
                   qt_ref, k_ref, vt_ref, gate_ref, causal_lanes, chunk):
    tm = h_ref.shape[1]
    chunks = [slice(c0, c0 + chunk) for c0 in range(0, tm, chunk)]

    def slot_table(t):
        rows = [jnp.zeros((QK_NOPE, chunk), t.dtype), t, t,
                jnp.zeros((HEAD_SLOT - QK_HEAD, chunk), t.dtype)]
        return jnp.concatenate(rows, axis=0).T

    ps = [_dot(_bf16(h_ref[0, rows, :]), w1_ref[...]) for rows in chunks]

    lora = []
    for rows, p in zip(chunks, ps):
        cqn = _bf16(_rms(p[:, :Q_LORA], gq_ref[...]))
        ckvn = _bf16(_rms(p[:, _OFF_CKV:_OFF_Z], gkv_ref[...]))
        gate_ref[0, rows, :] = _bf16(_silu(p[:, _OFF_Z:_OFF_ROPE]))
        cos_t = cos_t_ref[0, :, rows]
        sin_t = sin_t_ref[0, :, rows]
        lora.append((cqn, ckvn, cos_t, sin_t))

        ab = p[:, _OFF_ROPE:_W1_COLS]
        kr_even = ab * slot_table(cos_t) + pltpu.roll(ab, HEAD_SLOT - QK_ROPE, 1) * slot_table(sin_t)
        if causal_lanes:
            row = lax.broadcasted_iota(jnp.int32, kr_even.shape, 0)
            c = lax.broadcasted_iota(jnp.int32, kr_even.shape, 1) - QK_HEAD
            later_chunk = (c >= 0) & (c < chunk // CHUNK - 1) & (row >> CHUNK_SHIFT > c)
            kr_even = jnp.where(later_chunk, 1.0, kr_even)
        kr_odd = pltpu.roll(kr_even, ODD_SHIFT, 1)
        kn = _dot(ckvn, wuk_ref[...])
        low_half = lax.broadcasted_iota(jnp.int32, kr_even.shape, 1) < QK_NOPE
        for hd in range(0, MLA_HEADS, 2):
            pair = kn[:, hd * QK_NOPE:(hd + 2) * QK_NOPE]
            k_ref[0, hd, rows, :] = _bf16(jnp.where(low_half, pair, kr_even))
            k_ref[0, hd + 1, rows, :] = _bf16(jnp.where(low_half, kr_odd, pair))

    ones_row = (lax.broadcasted_iota(jnp.int32, (BF16_ROWS, chunk), 0) == 0).astype(jnp.bfloat16)
    for ci, (_, ckvn, _, _) in enumerate(lora):
        vt = _dot_nt(wuv_ref[...], ckvn)
        for hd in range(MLA_HEADS):
            vt_ref[0, ci, hd * V_SLOT:hd * V_SLOT + V_HEAD, :] = _bf16(vt[_head_rows(hd, V_HEAD)])
            vt_ref[0, ci, hd * V_SLOT + V_HEAD:(hd + 1) * V_SLOT, :] = ones_row

    scale = QK_HEAD ** -0.5 * LOG2_E
    for rows, (cqn, _, cos_t, sin_t) in zip(chunks, lora):
        qt = _dot_nt(wuq_ref[...], cqn) * scale
        for hd in range(MLA_HEADS):
            base = hd * QK_HEAD
            nope = _bf16(qt[base:base + QK_NOPE])
            x1 = qt[base + QK_NOPE:base + QK_NOPE + HALF_ROPE]
            x2 = qt[base + QK_NOPE + HALF_ROPE:base + QK_HEAD]
            rope = _bf16(jnp.concatenate([x1 * cos_t - x2 * sin_t, x1 * sin_t + x2 * cos_t], axis=0))
            if hd % 2 == 0:
                qt_ref[0, base:base + QK_NOPE, rows] = nope
                qt_ref[0, base + QK_NOPE:base + QK_HEAD, rows] = rope
            else:
                qt_ref[0, base:base + QK_ROPE, rows] = rope
                qt_ref[0, base + QK_ROPE:base + QK_HEAD, rows] = nope


def _mla_proj_kernel(h_ref, cos_ref, sin_ref, hm_ref, cosm_ref, sinm_ref,
                     w1_ref, gq_ref, gkv_ref, wuq_ref, wuk_ref, wuv_ref,
                     qt_ref, k_ref, vt_ref, gate_ref, qtm_ref, km_ref, vtm_ref, gatem_ref):
    weights = (w1_ref, gq_ref, gkv_ref, wuq_ref, wuk_ref, wuv_ref)

    @pl.when(_first_step())
    def _():
        _mla_proj_tile(hm_ref, cosm_ref, sinm_ref, *weights, qtm_ref, km_ref, vtm_ref, gatem_ref,
                       causal_lanes=False, chunk=META_PAD)

    _mla_proj_tile(h_ref, cos_ref, sin_ref, *weights, qt_ref, k_ref, vt_ref, gate_ref,
                   causal_lanes=True, chunk=TOKEN_TILE)


def _mla_proj(h, hm, tabs, meta_tabs, w):
    bx, lx, _ = h.shape
    tm = PROJ_TILE
    per_step = tm // TOKEN_TILE
    tile = lambda b, i: (b, i, 0)
    tile_t = lambda b, i: (b, 0, i)

    def outputs(nb, n_tok, n_tiles, t):
        return [
            jax.ShapeDtypeStruct((nb, QK_WIDTH, n_tok), jnp.bfloat16),
            jax.ShapeDtypeStruct((nb, MLA_HEADS, n_tok, HEAD_SLOT), jnp.bfloat16),
            jax.ShapeDtypeStruct((nb, n_tiles, V_WIDTH, t), jnp.bfloat16),
            jax.ShapeDtypeStruct((nb, n_tok, MLA_WIDTH), jnp.bfloat16),
        ]

    res = pl.pallas_call(
        _mla_proj_kernel,
        grid=(bx, lx // tm),
        in_specs=[
            pl.BlockSpec((1, tm, D_MODEL), tile),
            pl.BlockSpec((1, HALF_ROPE, tm), tile_t),
            pl.BlockSpec((1, HALF_ROPE, tm), tile_t),
            _const_spec((1, META_PAD, D_MODEL)),
            _const_spec((1, HALF_ROPE, META_PAD)),
            _const_spec((1, HALF_ROPE, META_PAD)),
            _const_spec((D_MODEL, _W1_COLS)),
            _const_spec((1, Q_LORA)),
            _const_spec((1, KV_LORA)),
            _const_spec((QK_WIDTH, Q_LORA)),
            _const_spec((KV_LORA, MLA_HEADS * QK_NOPE)),
            _const_spec((MLA_WIDTH, KV_LORA)),
        ],
        out_specs=[
            pl.BlockSpec((1, QK_WIDTH, tm), tile_t),
            pl.BlockSpec((1, MLA_HEADS, tm, HEAD_SLOT), lambda b, i: (b, 0, i, 0)),
            pl.BlockSpec((1, per_step, V_WIDTH, TOKEN_TILE), lambda b, i: (b, i, 0, 0)),
            pl.BlockSpec((1, tm, MLA_WIDTH), tile),
            _const_spec((1, QK_WIDTH, META_PAD)),
            _const_spec((1, MLA_HEADS, META_PAD, HEAD_SLOT)),
            _const_spec((1, 1, V_WIDTH, META_PAD)),
            _const_spec((1, META_PAD, MLA_WIDTH)),
        ],
        out_shape=outputs(bx, lx, lx // TOKEN_TILE, TOKEN_TILE) + outputs(1, META_PAD, 1, META_PAD),
        compiler_params=pltpu.CompilerParams(
            dimension_semantics=("arbitrary", "arbitrary"), vmem_limit_bytes=VMEM_LIMIT),
        name="mla_proj",
    )(h, *tabs, hm, *meta_tabs, w["w1"], w["gq"], w["gkv"], w["wuq_t"], w["wuk"], w["wuv_t"])
    return res[:4], res[4:]


def _softmax_start(s):
    m = jnp.max(s, axis=0, keepdims=True)
    return m, jnp.exp2(s - m)


def _normalize(acc):
    return acc[:V_HEAD] / acc[V_HEAD:V_HEAD + 1]


def _q_operand(q_ref, hd, mask):
    q = q_ref[0, _head_rows(hd, QK_HEAD), :]
    zero = jnp.zeros((BF16_ROWS, q.shape[1]), q.dtype)
    spare = [zero if mask is None else mask, zero]
    parts = [q] + spare if hd % 2 == 0 else [q[:QK_ROPE]] + spare + [q[QK_ROPE:]]
    return jnp.concatenate(parts, axis=0)


def _attn_kernel(qt_ref, qn_ref, kb_ref, vb_ref, qtm_ref, km_ref, vm_ref, ot_ref, otm_ref,
                 kbuf_ref, vbuf_ref, s_ref, s0_ref, sm0_ref, m_ref, acc_ref):
    tk = vb_ref.shape[3]
    tq = qt_ref.shape[2]
    qi = pl.program_id(1)

    kbuf_ref[:, pl.ds(pl.multiple_of(qi * tk, tk), tk), :] = kb_ref[0]
    vbuf_ref[qi] = vb_ref[0, 0]

    row = lax.broadcasted_iota(jnp.int32, (BF16_ROWS, tq), 0)
    qry = lax.broadcasted_iota(jnp.int32, (BF16_ROWS, tq), 1)
    hide = (qry >> CHUNK_SHIFT == row) & (row < tk // CHUNK - 1)

    def mask_rows(is_diagonal):
        return _bf16(jnp.where(hide & is_diagonal, NEG_INF, 0.0))

    def scores(q_ref, hd, j, mask):
        row0 = pl.multiple_of(j * tk, tk)
        return _dot(kbuf_ref[hd, pl.ds(row0, tk), :], _q_operand(q_ref, hd, mask))

    def update(hd, s, j):
        m = m_ref[hd]
        m_new = jnp.maximum(m, jnp.max(s, axis=0, keepdims=True))
        alpha = jnp.exp2(m - m_new)
        p = jnp.exp2(s - m_new)
        m_ref[hd] = m_new
        acc_ref[hd] = alpha * acc_ref[hd] + _dot(vbuf_ref[j, _head_rows(hd, V_SLOT), :], _bf16(p))

    def meta_scores(q_ref):
        return [_dot(km_ref[0, hd], _q_operand(q_ref, hd, None))
                for hd in range(MLA_HEADS)]

    def meta_values(hd, p):
        no_weight = jnp.zeros((META_PAD - N_META, p.shape[1]), jnp.bfloat16)
        return _dot(vm_ref[0, 0, _head_rows(hd, V_SLOT), :],
                    jnp.concatenate([_bf16(p), no_weight], axis=0))

    @pl.when(_first_step())
    def _():
        sm = meta_scores(qtm_ref)
        for hd in range(MLA_HEADS):
            _, p = _softmax_start(sm[hd])
            otm_ref[0, _head_rows(hd, V_HEAD), :] = _bf16(_normalize(meta_values(hd, p)))

    @pl.when(qi == 0)
    def _():
        sm = meta_scores(qt_ref)
        diag_mask = mask_rows(True)
        for hd in range(MLA_HEADS):
            s_ref[0, hd] = scores(qt_ref, hd, 0, diag_mask)
            m, p = _softmax_start(sm[hd])
            m_ref[hd] = m
            acc_ref[hd] = meta_values(hd, p)

    @pl.when(qi > 0)
    def _():
        first_mask = mask_rows(qi == 1)
        for hd in range(MLA_HEADS):
            s_ref[1, hd] = scores(qt_ref, hd, 1, first_mask)
            sm = sm0_ref[hd]
            s = s0_ref[hd]
            m = jnp.maximum(jnp.max(sm, axis=0, keepdims=True), jnp.max(s, axis=0, keepdims=True))
            m_ref[hd] = m
            acc_ref[hd] = meta_values(hd, jnp.exp2(sm - m)) + _dot(
                vbuf_ref[0, _head_rows(hd, V_SLOT), :], _bf16(jnp.exp2(s - m)))

    def pair(j):
        for hd in range(MLA_HEADS):
            s_ref[0, hd] = scores(qt_ref, hd, j + 1, None)
            update(hd, s_ref[1, hd], j)
        last_mask = mask_rows(j + 2 == qi)
        for hd in range(MLA_HEADS):
            s_ref[1, hd] = scores(qt_ref, hd, j + 2, last_mask)
            update(hd, s_ref[0, hd], j + 1)

    def quad_step(jj, carry):
        pair(4 * jj + 1)
        pair(4 * jj + 3)
        return carry

    n_mid = jnp.maximum(qi - 1, 0)
    lax.fori_loop(0, n_mid // 4, quad_step, 0)

    @pl.when(n_mid % 4 >= 2)
    def _():
        pair(4 * (n_mid // 4) + 1)

    @pl.when((qi >= 2) & (qi % 2 == 0))
    def _():
        diag_mask = mask_rows(True)
        for hd in range(MLA_HEADS):
            s_ref[0, hd] = scores(qt_ref, hd, qi, diag_mask)
            update(hd, s_ref[1, hd], qi - 1)

    for hd in range(MLA_HEADS):
        keys = jnp.concatenate([kbuf_ref[hd, 0:tk, :], km_ref[0, hd]], axis=0)
        s_next = _dot(keys, _q_operand(qn_ref, hd, None))
        s0_ref[hd] = s_next[:tk]
        sm0_ref[hd] = s_next[tk:]
        update(hd, s_ref[qi % 2, hd], qi)
    for hd in range(MLA_HEADS):
        ot_ref[0, _head_rows(hd, V_HEAD), :] = _bf16(_normalize(acc_ref[hd]))


def _attention(frames, meta):
    qt, k, vt, _ = frames
    qt_m, k_m, vt_m, _ = meta
    bx, _, lx = qt.shape
    nk, tk = vt.shape[1], vt.shape[3]
    tq = tk
    nq = lx // tq
    return pl.pallas_call(
        _attn_kernel,
        grid=(bx, nq),
        in_specs=[
            pl.BlockSpec((1, QK_WIDTH, tq), lambda b, i: (b, 0, i)),
            pl.BlockSpec((1, QK_WIDTH, tq), lambda b, i: (b, 0, jnp.minimum(i + 1, nq - 1))),
            pl.BlockSpec((1, MLA_HEADS, tk, HEAD_SLOT), lambda b, i: (b, 0, i, 0)),
            pl.BlockSpec((1, 1, V_WIDTH, tk), lambda b, i: (b, i, 0, 0)),
            _const_spec((1, QK_WIDTH, META_PAD)),
            _const_spec((1, MLA_HEADS, N_META, HEAD_SLOT)),
            _const_spec((1, 1, V_WIDTH, META_PAD)),
        ],
        out_specs=[
            pl.BlockSpec((1, MLA_WIDTH, tq), lambda b, i: (b, 0, i)),
            _const_spec((1, MLA_WIDTH, META_PAD)),
        ],
        out_shape=[
            jax.ShapeDtypeStruct((bx, MLA_WIDTH, lx), jnp.bfloat16),
            jax.ShapeDtypeStruct((1, MLA_WIDTH, META_PAD), jnp.bfloat16),
        ],
        scratch_shapes=[
            pltpu.VMEM((MLA_HEADS, lx, HEAD_SLOT), jnp.bfloat16),
            pltpu.VMEM((nk, V_WIDTH, tk), jnp.bfloat16),
            pltpu.VMEM((2, MLA_HEADS, tk, tq), jnp.float32),
            pltpu.VMEM((MLA_HEADS, tk, tq), jnp.float32),
            pltpu.VMEM((MLA_HEADS, N_META, tq), jnp.float32),
            pltpu.VMEM((MLA_HEADS, 1, tq), jnp.float32),
            pltpu.VMEM((MLA_HEADS, V_SLOT, tq), jnp.float32),
        ],
        compiler_params=pltpu.CompilerParams(
            dimension_semantics=("arbitrary", "arbitrary"), vmem_limit_bytes=VMEM_LIMIT),
        name="attention",
    )(qt, qt, k, vt, qt_m, k_m, vt_m)


def _out_conv_tile(ot_ref, gate_ref, h_ref, out_ref, cu_ref,
                   wo_ref, g1_ref, b1_ref, win_ref, cw_ref, wout_ref, g2_ref, b2_ref, chunk):
    tm = h_ref.shape[1]
    w = CONV_WIDTH
    chunks = [slice(c0, c0 + chunk) for c0 in range(0, tm, chunk)]

    attn = []
    for rows in chunks:
        o = ot_ref[0, :, rows].astype(jnp.float32).T
        y = _bf16(o * gate_ref[0, rows, :].astype(jnp.float32))
        attn.append(_dot(y, wo_ref[...]))

    hs, ys = [], []
    for rows, a in zip(chunks, attn):
        h = _layer_norm(DN_ALPHA * h_ref[0, rows, :] + a, g1_ref[...], b1_ref[...])
        hb = _bf16(h)
        cu = _dot(hb, win_ref[:, w:2 * w]) * _dot(hb, win_ref[:, 2 * w:3 * w])
        c0 = rows.start
        cu_ref[SUBLANES + c0:SUBLANES + c0 + chunk, :] = cu
        conv = cu_ref[SUBLANES - 2 + c0:SUBLANES - 2 + c0 + chunk, :] * cw_ref[0:1, :]
        conv = conv + cu_ref[SUBLANES - 1 + c0:SUBLANES - 1 + c0 + chunk, :] * cw_ref[1:2, :]
        conv = conv + cu * cw_ref[2:3, :]
        hs.append(h)
        ys.append(_dot(hb, win_ref[:, 0:w]) * conv * _silu(_dot(hb, win_ref[:, 3 * w:4 * w])))

    for rows, h, y in zip(chunks, hs, ys):
        r = DN_ALPHA * h + _dot(_bf16(y), wout_ref[...])
        out_ref[0, rows, :] = _layer_norm(r, g2_ref[...], b2_ref[...])


def _out_conv_kernel(ot_ref, gate_ref, h_ref, otm_ref, gatem_ref, hm_ref,
                     wo_ref, g1_ref, b1_ref, win_ref, cw_ref, wout_ref, g2_ref, b2_ref,
                     out_ref, outm_ref, cu_ref, meta_tail_ref):
    tm = h_ref.shape[1]
    weights = (wo_ref, g1_ref, b1_ref, win_ref, cw_ref, wout_ref, g2_ref, b2_ref)

    @pl.when(_first_step())
    def _():
        cu_ref[0:SUBLANES, :] = jnp.zeros((SUBLANES, CONV_WIDTH), jnp.float32)
        _out_conv_tile(otm_ref, gatem_ref, hm_ref, outm_ref, cu_ref, *weights, chunk=META_PAD)
        meta_tail_ref[...] = cu_ref[N_META:N_META + SUBLANES, :]

    @pl.when(pl.program_id(1) == 0)
    def _():
        cu_ref[0:SUBLANES, :] = meta_tail_ref[...]

    _out_conv_tile(ot_ref, gate_ref, h_ref, out_ref, cu_ref, *weights, chunk=TOKEN_TILE)
    cu_ref[0:SUBLANES, :] = cu_ref[tm:tm + SUBLANES, :]


def _out_conv(ot, gate, h, ot_m, gate_m, hm, w):
    bx, lx, _ = h.shape
    tm = CONV_TILE
    tile = lambda b_, i: (b_, i, 0)
    once = pl.Buffered(1)
    return pl.pallas_call(
        _out_conv_kernel,
        grid=(bx, lx // tm),
        in_specs=[
            pl.BlockSpec((1, MLA_WIDTH, tm), lambda b_, i: (b_, 0, i)),
            pl.BlockSpec((1, tm, MLA_WIDTH), tile),
            pl.BlockSpec((1, tm, D_MODEL), tile),
            _const_spec((1, MLA_WIDTH, META_PAD), once),
            _const_spec((1, META_PAD, MLA_WIDTH), once),
            _const_spec((1, META_PAD, D_MODEL), once),
            _const_spec((MLA_WIDTH, D_MODEL), once),
            _const_spec((1, D_MODEL), once),
            _const_spec((1, D_MODEL), once),
            _const_spec((D_MODEL, 4 * CONV_WIDTH), once),
            _const_spec((SUBLANES, CONV_WIDTH), once),
            _const_spec((CONV_WIDTH, D_MODEL), once),
            _const_spec((1, D_MODEL), once),
            _const_spec((1, D_MODEL), once),
        ],
        out_specs=[
            pl.BlockSpec((1, tm, D_MODEL), tile),
            _const_spec((1, META_PAD, D_MODEL)),
        ],
        out_shape=[
            jax.ShapeDtypeStruct(h.shape, jnp.float32),
            jax.ShapeDtypeStruct(hm.shape, jnp.float32),
        ],
        scratch_shapes=[
            pltpu.VMEM((tm + SUBLANES, CONV_WIDTH), jnp.float32),
            pltpu.VMEM((SUBLANES, CONV_WIDTH), jnp.float32),
        ],
        compiler_params=pltpu.CompilerParams(
            dimension_semantics=("arbitrary", "arbitrary"), vmem_limit_bytes=VMEM_LIMIT),
        name="out_conv",
    )(ot, gate, h, ot_m, gate_m, hm,
      w["wo"], w["g1"], w["b1"], w["win"], w["cw"], w["wout"], w["g2"], w["b2"])


def _prep_weights(j, ln_g, ln_b, w_in, gq, w_uq, gkv, w_uk, w_uv, w_o, conv_w_in, conv_w, conv_w_out):
    o_kr = Q_LORA + KV_LORA
    o_z = o_kr + QK_ROPE
    w = w_in[j]
    kr1 = _bf16(w[:, o_kr:o_kr + HALF_ROPE])
    kr2 = _bf16(w[:, o_kr + HALF_ROPE:o_z])
    zl = jnp.zeros((D_MODEL, QK_NOPE), jnp.bfloat16)
    return {
        "w1": jnp.concatenate([_bf16(w[:, :o_kr]), _bf16(w[:, o_z:]), zl, kr1, kr2, -kr2, kr1], axis=1),
        "gq": gq[j].reshape(1, Q_LORA),
        "gkv": gkv[j].reshape(1, KV_LORA),
        "wuq_t": _bf16(w_uq[j].T),
        "wuk": _bf16(w_uk[j]),
        "wuv_t": _bf16(w_uv[j].T),
        "wo": _bf16(w_o[j]),
        "g1": ln_g[2 * j].reshape(1, D_MODEL), "b1": ln_b[2 * j].reshape(1, D_MODEL),
        "g2": ln_g[2 * j + 1].reshape(1, D_MODEL), "b2": ln_b[2 * j + 1].reshape(1, D_MODEL),
        "win": _bf16(conv_w_in[j]), "wout": _bf16(conv_w_out[j]),
        "cw": jnp.pad(conv_w[j].astype(jnp.float32), ((0, SUBLANES - CONV_K), (0, 0))),
    }


def _rope_tables(pos):
    inv_freq = ROPE_BASE ** (-jnp.arange(0, QK_ROPE, 2, dtype=jnp.float32) / QK_ROPE)
    ang = pos[..., None] * inv_freq
    return jnp.swapaxes(jnp.cos(ang), 1, 2), jnp.swapaxes(jnp.sin(ang), 1, 2)


def kernel(x, positions, meta_tokens, ln_g, ln_b, mla_w_in, mla_q_norm_g, mla_w_uq, mla_kv_norm_g,
           mla_w_uk, mla_w_uv, mla_w_o, conv_w_in, conv_w, conv_w_out):
    bsz, seq, _ = x.shape
    assert seq % CONV_TILE == 0 and CONV_TILE % TOKEN_TILE == 0 and TOKEN_TILE % CHUNK == 0
    assert seq % PROJ_TILE == 0 and PROJ_TILE % TOKEN_TILE == 0
    assert DEPTH % 2 == 0
    f32 = jnp.float32

    hf = x.astype(f32)
    hm = jnp.pad(meta_tokens.astype(f32), ((0, META_PAD - N_META), (0, 0)))[None]

    frame_tabs = _rope_tables((positions + N_META).astype(f32))
    meta_pos = jnp.pad(jnp.arange(N_META, dtype=f32), (0, META_PAD - N_META))[None]
    meta_tabs = _rope_tables(meta_pos)

    for j in range(DEPTH // 2):
        w = _prep_weights(j, ln_g, ln_b, mla_w_in, mla_q_norm_g, mla_w_uq, mla_kv_norm_g,
                          mla_w_uk, mla_w_uv, mla_w_o, conv_w_in, conv_w, conv_w_out)
        frames, meta = _mla_proj(hf, hm, frame_tabs, meta_tabs, w)
        ot_f, ot_m = _attention(frames, meta)
        hf, hm = _out_conv(ot_f, frames[3], hf, ot_m, meta[3], hm, w)
    return hf
```

```python
import functools

import jax
import jax.numpy as jnp
from jax import lax
from jax.experimental import pallas as pl
from jax.experimental.pallas import tpu as pltpu

D_MODEL = 1024
DEPTH = 4
CHUNK = 64
CHUNK_SHIFT = CHUNK.bit_length() - 1
N_META = 16
MLA_HEADS = 16
QK_NOPE = 64
QK_ROPE = 32
HALF_ROPE = QK_ROPE // 2
V_HEAD = 64
Q_LORA = 384
KV_LORA = 256
MLA_WIDTH = MLA_HEADS * V_HEAD
ROPE_BASE = 10000.0
CONV_WIDTH = D_MODEL
CONV_K = 3
DN_ALPHA = (2 * DEPTH) ** 0.25
LN_EPS = 1e-5
RMS_EPS = 1e-6
NEG_INF = -1e30
LOG2_E = 1.4426950408889634

QK_HEAD = QK_NOPE + QK_ROPE
QK_WIDTH = MLA_HEADS * QK_HEAD
HEAD_SLOT = 128
ODD_SHIFT = HEAD_SLOT - QK_NOPE
BF16_ROWS = 16
V_SLOT = V_HEAD + BF16_ROWS
V_WIDTH = MLA_HEADS * V_SLOT
META_PAD = 128
TOKEN_TILE = 256
CONV_TILE = 512
PROJ_TILE = 512
SUBLANES = 8
VMEM_LIMIT = 56 * 1024 * 1024

_OFF_CKV = Q_LORA
_OFF_Z = Q_LORA + KV_LORA
_OFF_ROPE = _OFF_Z + MLA_WIDTH
_W1_COLS = _OFF_ROPE + HEAD_SLOT

_NT = (((1,), (1,)), ((), ()))


def _bf16(x):
    return x.astype(jnp.bfloat16)


def _dot(a, b):
    return jnp.dot(a, b, preferred_element_type=jnp.float32)


def _dot_nt(a, b):
    return lax.dot_general(a, b, _NT, preferred_element_type=jnp.float32)


def _rms(x, g):
    y = x * lax.rsqrt(jnp.mean(jnp.square(x), axis=-1, keepdims=True) + RMS_EPS)
    return y * g


def _layer_norm(x, g, b):
    mu = jnp.mean(x, axis=-1, keepdims=True)
    xc = x - mu
    var = jnp.mean(jnp.square(xc), axis=-1, keepdims=True)
    return xc * lax.rsqrt(var + LN_EPS) * g + b


def _silu(z):
    return z * (1.0 / (1.0 + jnp.exp(-z)))


def _head_rows(hd, width):
    return slice(hd * width, (hd + 1) * width)


def _first_step():
    return (pl.program_id(0) == 0) & (pl.program_id(1) == 0)


def _const_spec(shape, pipeline_mode=None):
    return pl.BlockSpec(shape, lambda b, i: (0,) * len(shape), pipeline_mode=pipeline_mode)


def _layer_spec(layer, shape, pipeline_mode=None):
    return pl.BlockSpec((None,) + shape, lambda b, i: (layer,) + (0,) * len(shape),
                        pipeline_mode=pipeline_mode)


def _mla_proj_tile(h_ref, cos_t_ref, sin_t_ref, w1_ref, gq_ref, gkv_ref, wuq_ref, wuk_ref, wuv_ref,
                   qt_ref, k_ref, vt_ref, gate_ref, causal_lanes, chunk):
    tm = h_ref.shape[1]
    chunks = [slice(c0, c0 + chunk) for c0 in range(0, tm, chunk)]

    def slot_table(t):
        rows = [jnp.zeros((QK_NOPE, chunk), t.dtype), t, t,
                jnp.zeros((HEAD_SLOT - QK_HEAD, chunk), t.dtype)]
        return jnp.concatenate(rows, axis=0).T

    ps = [_dot(_bf16(h_ref[0, rows, :]), w1_ref[...]) for rows in chunks]

    lora = []
    for rows, p in zip(chunks, ps):
        cqn = _bf16(_rms(p[:, :Q_LORA], gq_ref[...]))
        ckvn = _bf16(_rms(p[:, _OFF_CKV:_OFF_Z], gkv_ref[...]))
        gate_ref[0, rows, :] = _bf16(_silu(p[:, _OFF_Z:_OFF_ROPE]))
        cos_t = cos_t_ref[0, :, rows]
        sin_t = sin_t_ref[0, :, rows]
        lora.append((cqn, ckvn, cos_t, sin_t))

        ab = p[:, _OFF_ROPE:_W1_COLS]
        kr_even = ab * slot_table(cos_t) + pltpu.roll(ab, HEAD_SLOT - QK_ROPE, 1) * slot_table(sin_t)
        if causal_lanes:
            row = lax.broadcasted_iota(jnp.int32, kr_even.shape, 0)
            c = lax.broadcasted_iota(jnp.int32, kr_even.shape, 1) - QK_HEAD
            later_chunk = (c >= 0) & (c < chunk // CHUNK - 1) & (row >> CHUNK_SHIFT > c)
            kr_even = jnp.where(later_chunk, 1.0, kr_even)
        kr_odd = pltpu.roll(kr_even, ODD_SHIFT, 1)
        kn = _dot(ckvn, wuk_ref[...])
        low_half = lax.broadcasted_iota(jnp.int32, kr_even.shape, 1) < QK_NOPE
        for hd in range(0, MLA_HEADS, 2):
            pair = kn[:, hd * QK_NOPE:(hd + 2) * QK_NOPE]
            k_ref[0, hd, rows, :] = _bf16(jnp.where(low_half, pair, kr_even))
            k_ref[0, hd + 1, rows, :] = _bf16(jnp.where(low_half, kr_odd, pair))

    ones_row = (lax.broadcasted_iota(jnp.int32, (BF16_ROWS, chunk), 0) == 0).astype(jnp.bfloat16)
    for ci, (_, ckvn, _, _) in enumerate(lora):
        vt = _dot_nt(wuv_ref[...], ckvn)
        for hd in range(MLA_HEADS):
            vt_ref[0, ci, hd * V_SLOT:hd * V_SLOT + V_HEAD, :] = _bf16(vt[_head_rows(hd, V_HEAD)])
            vt_ref[0, ci, hd * V_SLOT + V_HEAD:(hd + 1) * V_SLOT, :] = ones_row

    scale = QK_HEAD ** -0.5 * LOG2_E
    for rows, (cqn, _, cos_t, sin_t) in zip(chunks, lora):
        qt = _dot_nt(wuq_ref[...], cqn) * scale
        for hd in range(MLA_HEADS):
            base = hd * QK_HEAD
            nope = _bf16(qt[base:base + QK_NOPE])
            x1 = qt[base + QK_NOPE:base + QK_NOPE + HALF_ROPE]
            x2 = qt[base + QK_NOPE + HALF_ROPE:base + QK_HEAD]
            rope = _bf16(jnp.concatenate([x1 * cos_t - x2 * sin_t, x1 * sin_t + x2 * cos_t], axis=0))
            if hd % 2 == 0:
                qt_ref[0, base:base + QK_NOPE, rows] = nope
                qt_ref[0, base + QK_NOPE:base + QK_HEAD, rows] = rope
            else:
                qt_ref[0, base:base + QK_ROPE, rows] = rope
                qt_ref[0, base + QK_ROPE:base + QK_HEAD, rows] = nope


def _mla_proj_kernel(h_ref, cos_ref, sin_ref, hm_ref, cosm_ref, sinm_ref,
                     w1_ref, gq_ref, gkv_ref, wuq_ref, wuk_ref, wuv_ref,
                     qt_ref, k_ref, vt_ref, gate_ref, qtm_ref, km_ref, vtm_ref, gatem_ref, *, layer):
    row = slice(layer, layer + 1)
    weights = (w1_ref, gq_ref.at[row], gkv_ref.at[row], wuq_ref, wuk_ref, wuv_ref)

    @pl.when(_first_step())
    def _():
        _mla_proj_tile(hm_ref, cosm_ref, sinm_ref, *weights, qtm_ref, km_ref, vtm_ref, gatem_ref,
                       causal_lanes=False, chunk=META_PAD)

    _mla_proj_tile(h_ref, cos_ref, sin_ref, *weights, qt_ref, k_ref, vt_ref, gate_ref,
                   causal_lanes=True, chunk=TOKEN_TILE)


def _mla_proj(h, hm, tabs, meta_tabs, w, layer):
    bx, lx, _ = h.shape
    tm = PROJ_TILE
    per_step = tm // TOKEN_TILE
    tile = lambda b, i: (b, i, 0)
    tile_t = lambda b, i: (b, 0, i)

    def outputs(nb, n_tok, n_tiles, t):
        return [
            jax.ShapeDtypeStruct((nb, QK_WIDTH, n_tok), jnp.bfloat16),
            jax.ShapeDtypeStruct((nb, MLA_HEADS, n_tok, HEAD_SLOT), jnp.bfloat16),
            jax.ShapeDtypeStruct((nb, n_tiles, V_WIDTH, t), jnp.bfloat16),
            jax.ShapeDtypeStruct((nb, n_tok, MLA_WIDTH), jnp.bfloat16),
        ]

    res = pl.pallas_call(
        functools.partial(_mla_proj_kernel, layer=layer),
        grid=(bx, lx // tm),
        in_specs=[
            pl.BlockSpec((1, tm, D_MODEL), tile),
            pl.BlockSpec((1, HALF_ROPE, tm), tile_t),
            pl.BlockSpec((1, HALF_ROPE, tm), tile_t),
            _const_spec((1, META_PAD, D_MODEL)),
            _const_spec((1, HALF_ROPE, META_PAD)),
            _const_spec((1, HALF_ROPE, META_PAD)),
            _const_spec((D_MODEL, _W1_COLS)),
            _const_spec(w["gq"].shape),
            _const_spec(w["gkv"].shape),
            _layer_spec(layer, (QK_WIDTH, Q_LORA)),
            _layer_spec(layer, (KV_LORA, MLA_HEADS * QK_NOPE)),
            _layer_spec(layer, (MLA_WIDTH, KV_LORA)),
        ],
        out_specs=[
            pl.BlockSpec((1, QK_WIDTH, tm), tile_t),
            pl.BlockSpec((1, MLA_HEADS, tm, HEAD_SLOT), lambda b, i: (b, 0, i, 0)),
            pl.BlockSpec((1, per_step, V_WIDTH, TOKEN_TILE), lambda b, i: (b, i, 0, 0)),
            pl.BlockSpec((1, tm, MLA_WIDTH), tile),
            _const_spec((1, QK_WIDTH, META_PAD)),
            _const_spec((1, MLA_HEADS, META_PAD, HEAD_SLOT)),
            _const_spec((1, 1, V_WIDTH, META_PAD)),
            _const_spec((1, META_PAD, MLA_WIDTH)),
        ],
        out_shape=outputs(bx, lx, lx // TOKEN_TILE, TOKEN_TILE) + outputs(1, META_PAD, 1, META_PAD),
        compiler_params=pltpu.CompilerParams(
            dimension_semantics=("arbitrary", "arbitrary"), vmem_limit_bytes=VMEM_LIMIT),
        name="mla_proj",
    )(h, *tabs, hm, *meta_tabs, w["w1"], w["gq"], w["gkv"], w["wuq_t"], w["wuk"], w["wuv_t"])
    return res[:4], res[4:]


def _softmax_start(s):
    m = jnp.max(s, axis=0, keepdims=True)
    return m, jnp.exp2(s - m)


def _normalize(acc):
    return acc[:V_HEAD] / acc[V_HEAD:V_HEAD + 1]


def _q_operand(q_ref, hd, mask):
    q = q_ref[0, _head_rows(hd, QK_HEAD), :]
    zero = jnp.zeros((BF16_ROWS, q.shape[1]), q.dtype)
    spare = [zero if mask is None else mask, zero]
    parts = [q] + spare if hd % 2 == 0 else [q[:QK_ROPE]] + spare + [q[QK_ROPE:]]
    return jnp.concatenate(parts, axis=0)


def _attn_kernel(qt_ref, qn_ref, kb_ref, vb_ref, qtm_ref, km_ref, vm_ref, *rest, convert):
    if convert:
        _convert_rows(*rest[:4], *rest[6:10])
        rest = rest[4:6] + rest[10:]
    _attn_step(qt_ref, qn_ref, kb_ref, vb_ref, qtm_ref, km_ref, vm_ref, *rest)


def _attn_step(qt_ref, qn_ref, kb_ref, vb_ref, qtm_ref, km_ref, vm_ref, ot_ref, otm_ref,
               kbuf_ref, vbuf_ref, s_ref, s0_ref, sm0_ref, m_ref, acc_ref):
    tk = vb_ref.shape[3]
    tq = qt_ref.shape[2]
    qi = pl.program_id(1)

    kbuf_ref[:, pl.ds(pl.multiple_of(qi * tk, tk), tk), :] = kb_ref[0]
    vbuf_ref[qi] = vb_ref[0, 0]

    row = lax.broadcasted_iota(jnp.int32, (BF16_ROWS, tq), 0)
    qry = lax.broadcasted_iota(jnp.int32, (BF16_ROWS, tq), 1)
    hide = (qry >> CHUNK_SHIFT == row) & (row < tk // CHUNK - 1)

    def mask_rows(is_diagonal):
        return _bf16(jnp.where(hide & is_diagonal, NEG_INF, 0.0))

    def scores(q_ref, hd, j, mask):
        row0 = pl.multiple_of(j * tk, tk)
        return _dot(kbuf_ref[hd, pl.ds(row0, tk), :], _q_operand(q_ref, hd, mask))

    def update(hd, s, j):
        m = m_ref[hd]
        m_new = jnp.maximum(m, jnp.max(s, axis=0, keepdims=True))
        alpha = jnp.exp2(m - m_new)
        p = jnp.exp2(s - m_new)
        m_ref[hd] = m_new
        acc_ref[hd] = alpha * acc_ref[hd] + _dot(vbuf_ref[j, _head_rows(hd, V_SLOT), :], _bf16(p))

    def meta_scores(q_ref):
        return [_dot(km_ref[0, hd], _q_operand(q_ref, hd, None))
                for hd in range(MLA_HEADS)]

    def meta_values(hd, p):
        no_weight = jnp.zeros((META_PAD - N_META, p.shape[1]), jnp.bfloat16)
        return _dot(vm_ref[0, 0, _head_rows(hd, V_SLOT), :],
                    jnp.concatenate([_bf16(p), no_weight], axis=0))

    @pl.when(_first_step())
    def _():
        sm = meta_scores(qtm_ref)
        for hd in range(MLA_HEADS):
            _, p = _softmax_start(sm[hd])
            otm_ref[0, _head_rows(hd, V_HEAD), :] = _bf16(_normalize(meta_values(hd, p)))

    @pl.when(qi == 0)
    def _():
        sm = meta_scores(qt_ref)
        diag_mask = mask_rows(True)
        for hd in range(MLA_HEADS):
            s_ref[0, hd] = scores(qt_ref, hd, 0, diag_mask)
            m, p = _softmax_start(sm[hd])
            m_ref[hd] = m
            acc_ref[hd] = meta_values(hd, p)

    @pl.when(qi > 0)
    def _():
        first_mask = mask_rows(qi == 1)
        for hd in range(MLA_HEADS):
            s_ref[1, hd] = scores(qt_ref, hd, 1, first_mask)
            sm = sm0_ref[hd]
            s = s0_ref[hd]
            m = jnp.maximum(jnp.max(sm, axis=0, keepdims=True), jnp.max(s, axis=0, keepdims=True))
            m_ref[hd] = m
            acc_ref[hd] = meta_values(hd, jnp.exp2(sm - m)) + _dot(
                vbuf_ref[0, _head_rows(hd, V_SLOT), :], _bf16(jnp.exp2(s - m)))

    def pair(j):
        for hd in range(MLA_HEADS):
            s_ref[0, hd] = scores(qt_ref, hd, j + 1, None)
            update(hd, s_ref[1, hd], j)
        last_mask = mask_rows(j + 2 == qi)
        for hd in range(MLA_HEADS):
            s_ref[1, hd] = scores(qt_ref, hd, j + 2, last_mask)
            update(hd, s_ref[0, hd], j + 1)

    def quad_step(jj, carry):
        pair(4 * jj + 1)
        pair(4 * jj + 3)
        return carry

    n_mid = jnp.maximum(qi - 1, 0)
    lax.fori_loop(0, n_mid // 4, quad_step, 0)

    @pl.when(n_mid % 4 >= 2)
    def _():
        pair(4 * (n_mid // 4) + 1)

    @pl.when((qi >= 2) & (qi % 2 == 0))
    def _():
        diag_mask = mask_rows(True)
        for hd in range(MLA_HEADS):
            s_ref[0, hd] = scores(qt_ref, hd, qi, diag_mask)
            update(hd, s_ref[1, hd], qi - 1)

    for hd in range(MLA_HEADS):
        keys = jnp.concatenate([kbuf_ref[hd, 0:tk, :], km_ref[0, hd]], axis=0)
        s_next = _dot(keys, _q_operand(qn_ref, hd, None))
        s0_ref[hd] = s_next[:tk]
        sm0_ref[hd] = s_next[tk:]
        update(hd, s_ref[qi % 2, hd], qi)
    for hd in range(MLA_HEADS):
        ot_ref[0, _head_rows(hd, V_HEAD), :] = _bf16(_normalize(acc_ref[hd]))


def _attention(frames, meta, next_params=None):
    qt, k, vt, _ = frames
    qt_m, k_m, vt_m, _ = meta
    bx, _, lx = qt.shape
    nk, tk = vt.shape[1], vt.shape[3]
    tq = tk
    nq = lx // tq
    extra_in, extra_specs, extra_out_specs, extra_out_shape = [], [], [], []
    if next_params is not None:
        layer, *extra_in = next_params
        assert bx * nq * CONVERT_ROWS == D_MODEL
        step = lambda b, i: b * nq + i
        extra_specs = [pl.BlockSpec((None, CONVERT_ROWS, p.shape[2]), lambda b, i: (layer, step(b, i), 0))
                       for p in extra_in]
        extra_out_specs = [pl.BlockSpec(sh, lambda b, i: (step(b, i), 0)) for sh in _big_shapes(CONVERT_ROWS)]
        extra_out_shape = [jax.ShapeDtypeStruct(sh, jnp.bfloat16) for sh in _big_shapes(D_MODEL)]
    res = pl.pallas_call(
        functools.partial(_attn_kernel, convert=next_params is not None),
        grid=(bx, nq),
        in_specs=[
            pl.BlockSpec((1, QK_WIDTH, tq), lambda b, i: (b, 0, i)),
            pl.BlockSpec((1, QK_WIDTH, tq), lambda b, i: (b, 0, jnp.minimum(i + 1, nq - 1))),
            pl.BlockSpec((1, MLA_HEADS, tk, HEAD_SLOT), lambda b, i: (b, 0, i, 0)),
            pl.BlockSpec((1, 1, V_WIDTH, tk), lambda b, i: (b, i, 0, 0)),
            _const_spec((1, QK_WIDTH, META_PAD)),
            _const_spec((1, MLA_HEADS, N_META, HEAD_SLOT)),
            _const_spec((1, 1, V_WIDTH, META_PAD)),
        ] + extra_specs,
        out_specs=[
            pl.BlockSpec((1, MLA_WIDTH, tq), lambda b, i: (b, 0, i)),
            _const_spec((1, MLA_WIDTH, META_PAD)),
        ] + extra_out_specs,
        out_shape=[
            jax.ShapeDtypeStruct((bx, MLA_WIDTH, lx), jnp.bfloat16),
            jax.ShapeDtypeStruct((1, MLA_WIDTH, META_PAD), jnp.bfloat16),
        ] + extra_out_shape,
        scratch_shapes=[
            pltpu.VMEM((MLA_HEADS, lx, HEAD_SLOT), jnp.bfloat16),
            pltpu.VMEM((nk, V_WIDTH, tk), jnp.bfloat16),
            pltpu.VMEM((2, MLA_HEADS, tk, tq), jnp.float32),
            pltpu.VMEM((MLA_HEADS, tk, tq), jnp.float32),
            pltpu.VMEM((MLA_HEADS, N_META, tq), jnp.float32),
            pltpu.VMEM((MLA_HEADS, 1, tq), jnp.float32),
            pltpu.VMEM((MLA_HEADS, V_SLOT, tq), jnp.float32),
        ],
        compiler_params=pltpu.CompilerParams(
            dimension_semantics=("arbitrary", "arbitrary"), vmem_limit_bytes=VMEM_LIMIT),
        name="attention",
    )(qt, qt, k, vt, qt_m, k_m, vt_m, *extra_in)
    return res[0], res[1], tuple(res[2:])


def _out_conv_tile(ot_ref, gate_ref, h_ref, out_ref, cu_ref,
                   wo_ref, g1_ref, b1_ref, win_ref, cw_ref, wout_ref, g2_ref, b2_ref, chunk):
    tm = h_ref.shape[1]
    w = CONV_WIDTH
    chunks = [slice(c0, c0 + chunk) for c0 in range(0, tm, chunk)]

    attn = []
    for rows in chunks:
        o = ot_ref[0, :, rows].astype(jnp.float32).T
        y = _bf16(o * gate_ref[0, rows, :].astype(jnp.float32))
        attn.append(_dot(y, wo_ref[...]))

    hs, ys = [], []
    for rows, a in zip(chunks, attn):
        h = _layer_norm(DN_ALPHA * h_ref[0, rows, :] + a, g1_ref[...], b1_ref[...])
        hb = _bf16(h)
        cu = _dot(hb, win_ref[:, w:2 * w]) * _dot(hb, win_ref[:, 2 * w:3 * w])
        c0 = rows.start
        cu_ref[SUBLANES + c0:SUBLANES + c0 + chunk, :] = cu
        conv = cu_ref[SUBLANES - 2 + c0:SUBLANES - 2 + c0 + chunk, :] * cw_ref[0:1, :]
        conv = conv + cu_ref[SUBLANES - 1 + c0:SUBLANES - 1 + c0 + chunk, :] * cw_ref[1:2, :]
        conv = conv + cu * cw_ref[2:3, :]
        hs.append(h)
        ys.append(_dot(hb, win_ref[:, 0:w]) * conv * _silu(_dot(hb, win_ref[:, 3 * w:4 * w])))

    for rows, h, y in zip(chunks, hs, ys):
        r = DN_ALPHA * h + _dot(_bf16(y), wout_ref[...])
        out_ref[0, rows, :] = _layer_norm(r, g2_ref[...], b2_ref[...])


def _out_conv_kernel(ot_ref, gate_ref, h_ref, otm_ref, gatem_ref, hm_ref,
                     wo_ref, lng_ref, lnb_ref, win_ref, cw_ref, wout_ref,
                     out_ref, outm_ref, cu_ref, meta_tail_ref, *, layer):
    tm = h_ref.shape[1]
    mla_row = slice(2 * layer, 2 * layer + 1)
    conv_row = slice(2 * layer + 1, 2 * layer + 2)
    weights = (wo_ref, lng_ref.at[mla_row], lnb_ref.at[mla_row], win_ref, cw_ref, wout_ref,
               lng_ref.at[conv_row], lnb_ref.at[conv_row])

    @pl.when(_first_step())
    def _():
        cu_ref[0:SUBLANES, :] = jnp.zeros((SUBLANES, CONV_WIDTH), jnp.float32)
        _out_conv_tile(otm_ref, gatem_ref, hm_ref, outm_ref, cu_ref, *weights, chunk=META_PAD)
        meta_tail_ref[...] = cu_ref[N_META:N_META + SUBLANES, :]

    @pl.when(pl.program_id(1) == 0)
    def _():
        cu_ref[0:SUBLANES, :] = meta_tail_ref[...]

    _out_conv_tile(ot_ref, gate_ref, h_ref, out_ref, cu_ref, *weights, chunk=TOKEN_TILE)
    cu_ref[0:SUBLANES, :] = cu_ref[tm:tm + SUBLANES, :]


def _out_conv(ot, gate, h, ot_m, gate_m, hm, w, layer):
    bx, lx, _ = h.shape
    tm = CONV_TILE
    tile = lambda b_, i: (b_, i, 0)
    once = pl.Buffered(1)
    return pl.pallas_call(
        functools.partial(_out_conv_kernel, layer=layer),
        grid=(bx, lx // tm),
        in_specs=[
            pl.BlockSpec((1, MLA_WIDTH, tm), lambda b_, i: (b_, 0, i)),
            pl.BlockSpec((1, tm, MLA_WIDTH), tile),
            pl.BlockSpec((1, tm, D_MODEL), tile),
            _const_spec((1, MLA_WIDTH, META_PAD), once),
            _const_spec((1, META_PAD, MLA_WIDTH), once),
            _const_spec((1, META_PAD, D_MODEL), once),
            _const_spec((MLA_WIDTH, D_MODEL), once),
            _const_spec(w["ln_g"].shape, once),
            _const_spec(w["ln_b"].shape, once),
            _const_spec((D_MODEL, 4 * CONV_WIDTH), once),
            _layer_spec(layer, (CONV_K, CONV_WIDTH), once),
            _const_spec((CONV_WIDTH, D_MODEL), once),
        ],
        out_specs=[
            pl.BlockSpec((1, tm, D_MODEL), tile),
            _const_spec((1, META_PAD, D_MODEL)),
        ],
        out_shape=[
            jax.ShapeDtypeStruct(h.shape, jnp.float32),
            jax.ShapeDtypeStruct(hm.shape, jnp.float32),
        ],
        scratch_shapes=[
            pltpu.VMEM((tm + SUBLANES, CONV_WIDTH), jnp.float32),
            pltpu.VMEM((SUBLANES, CONV_WIDTH), jnp.float32),
        ],
        compiler_params=pltpu.CompilerParams(
            dimension_semantics=("arbitrary", "arbitrary"), vmem_limit_bytes=VMEM_LIMIT),
        name="out_conv",
    )(ot, gate, h, ot_m, gate_m, hm,
      w["wo"], w["ln_g"], w["ln_b"], w["win"], w["cw"], w["wout"])


PREP_ROWS = 256
CONVERT_ROWS = 16


def _convert_rows(w_in_ref, wo_ref, cwin_ref, cwout_ref, w1_out, wo_out, win_out, wout_out):
    o_kr = Q_LORA + KV_LORA
    o_z = o_kr + QK_ROPE
    x = w_in_ref[...]
    w1_out[:, :o_kr] = _bf16(x[:, :o_kr])
    w1_out[:, _OFF_Z:_OFF_ROPE] = _bf16(x[:, o_z:])
    x1 = x[:, o_kr:o_kr + HALF_ROPE]
    x2 = x[:, o_kr + HALF_ROPE:o_z]
    rope = jnp.concatenate([jnp.zeros((x.shape[0], QK_NOPE), x.dtype), x1, x2, -x2, x1], axis=1)
    w1_out[:, _OFF_ROPE:_W1_COLS] = _bf16(rope)
    wo_out[...] = _bf16(wo_ref[...])
    win_out[...] = _bf16(cwin_ref[...])
    wout_out[...] = _bf16(cwout_ref[...])


def _prep_kernel(w_in_ref, wo_ref, cwin_ref, cwout_ref, wuq_ref, wuk_ref, wuv_ref,
                 w1_out, wo_out, win_out, wout_out, wuq_t_out, wuk_out, wuv_t_out):
    _convert_rows(w_in_ref, wo_ref, cwin_ref, cwout_ref, w1_out, wo_out, win_out, wout_out)

    @pl.when(pl.program_id(0) <= 1)
    def _():
        wuq_t_out[...] = _bf16(wuq_ref[...].T)
        wuk_out[...] = _bf16(wuk_ref[...])
        wuv_t_out[...] = _bf16(wuv_ref[...].T)


def _big_shapes(rows):
    return [(rows, _W1_COLS), (rows, D_MODEL), (rows, 4 * CONV_WIDTH), (rows, D_MODEL)]


def _prep_weights(w_in, w_uq, w_uk, w_uv, w_o, conv_w_in, conv_w_out):
    n = w_in.shape[0]
    assert n == 2
    rows = lambda cols: pl.BlockSpec((None, PREP_ROWS, cols), lambda r: (0, r, 0))
    rows_out = lambda shape: pl.BlockSpec(shape, lambda r: (r, 0))
    small = lambda shape: pl.BlockSpec((None,) + shape, lambda r: (jnp.minimum(r, 1), 0, 0))
    bf = lambda *shape: jax.ShapeDtypeStruct(shape, jnp.bfloat16)
    res = pl.pallas_call(
        _prep_kernel,
        grid=(D_MODEL // PREP_ROWS,),
        in_specs=[rows(w_in.shape[2]), rows(D_MODEL), rows(4 * CONV_WIDTH), rows(D_MODEL),
                  small((Q_LORA, QK_WIDTH)), small((KV_LORA, MLA_HEADS * QK_NOPE)),
                  small((KV_LORA, MLA_WIDTH))],
        out_specs=[rows_out(sh) for sh in _big_shapes(PREP_ROWS)] + [
            small((QK_WIDTH, Q_LORA)), small((KV_LORA, MLA_HEADS * QK_NOPE)), small((MLA_WIDTH, KV_LORA))],
        out_shape=[bf(*sh) for sh in _big_shapes(D_MODEL)] + [
            bf(n, QK_WIDTH, Q_LORA), bf(n, KV_LORA, MLA_HEADS * QK_NOPE), bf(n, MLA_WIDTH, KV_LORA)],
        compiler_params=pltpu.CompilerParams(
            dimension_semantics=("arbitrary",), vmem_limit_bytes=VMEM_LIMIT),
        name="weight_prep",
    )(w_in, w_o, conv_w_in, conv_w_out, w_uq, w_uk, w_uv)
    return tuple(res[:4]), {"wuq_t": res[4], "wuk": res[5], "wuv_t": res[6]}


def _rope_tables(pos):
    inv_freq = ROPE_BASE ** (-jnp.arange(0, QK_ROPE, 2, dtype=jnp.float32) / QK_ROPE)
    ang = pos[..., None] * inv_freq
    return jnp.swapaxes(jnp.cos(ang), 1, 2), jnp.swapaxes(jnp.sin(ang), 1, 2)


def kernel(x, positions, meta_tokens, ln_g, ln_b, mla_w_in, mla_q_norm_g, mla_w_uq, mla_kv_norm_g,
           mla_w_uk, mla_w_uv, mla_w_o, conv_w_in, conv_w, conv_w_out):
    bsz, seq, _ = x.shape
    assert seq % CONV_TILE == 0 and CONV_TILE % TOKEN_TILE == 0 and TOKEN_TILE % CHUNK == 0
    assert seq % PROJ_TILE == 0 and PROJ_TILE % TOKEN_TILE == 0
    assert DEPTH % 2 == 0
    f32 = jnp.float32

    hf = x.astype(f32)
    hm = jnp.pad(meta_tokens.astype(f32), ((0, META_PAD - N_META), (0, 0)))[None]

    frame_tabs = _rope_tables((positions + N_META).astype(f32))
    meta_pos = jnp.pad(jnp.arange(N_META, dtype=f32), (0, META_PAD - N_META))[None]
    meta_tabs = _rope_tables(meta_pos)

    big, w = _prep_weights(mla_w_in, mla_w_uq, mla_w_uk, mla_w_uv, mla_w_o, conv_w_in, conv_w_out)
    w.update({"gq": mla_q_norm_g, "gkv": mla_kv_norm_g, "ln_g": ln_g, "ln_b": ln_b, "cw": conv_w})
    n_pairs = DEPTH // 2
    for j in range(n_pairs):
        w.update(dict(zip(("w1", "wo", "win", "wout"), big)))
        frames, meta = _mla_proj(hf, hm, frame_tabs, meta_tabs, w, j)
        following = (j + 1, mla_w_in, mla_w_o, conv_w_in, conv_w_out) if j + 1 < n_pairs else None
        ot_f, ot_m, big_next = _attention(frames, meta, following)
        hf, hm = _out_conv(ot_f, frames[3], hf, ot_m, meta[3], hm, w, j)
        big = big_next
    return hf
```

```python
import functools

import jax
import jax.numpy as jnp
from jax import lax
from jax.experimental import pallas as pl
from jax.experimental.pallas import tpu as pltpu

D_MODEL = 1024
DEPTH = 4
CHUNK = 64
CHUNK_SHIFT = CHUNK.bit_length() - 1
N_META = 16
MLA_HEADS = 16
QK_NOPE = 64
QK_ROPE = 32
HALF_ROPE = QK_ROPE // 2
V_HEAD = 64
Q_LORA = 384
KV_LORA = 256
MLA_WIDTH = MLA_HEADS * V_HEAD
ROPE_BASE = 10000.0
CONV_WIDTH = D_MODEL
CONV_K = 3
DN_ALPHA = (2 * DEPTH) ** 0.25
LN_EPS = 1e-5
RMS_EPS = 1e-6
NEG_INF = -1e30
LOG2_E = 1.4426950408889634

QK_HEAD = QK_NOPE + QK_ROPE
QK_WIDTH = MLA_HEADS * QK_HEAD
HEAD_SLOT = 128
ODD_SHIFT = HEAD_SLOT - QK_NOPE
BF16_ROWS = 16
V_SLOT = V_HEAD + BF16_ROWS
V_WIDTH = MLA_HEADS * V_SLOT
META_PAD = 128
TOKEN_TILE = 256
CONV_TILE = 512
PROJ_TILE = 512
SUBLANES = 8
VMEM_LIMIT = 56 * 1024 * 1024

_OFF_CKV = Q_LORA
_OFF_Z = Q_LORA + KV_LORA
_OFF_ROPE = _OFF_Z + MLA_WIDTH
_W1_COLS = _OFF_ROPE + HEAD_SLOT

_NT = (((1,), (1,)), ((), ()))


def _bf16(x):
    return x.astype(jnp.bfloat16)


def _dot(a, b):
    return jnp.dot(a, b, preferred_element_type=jnp.float32)


def _dot_nt(a, b):
    return lax.dot_general(a, b, _NT, preferred_element_type=jnp.float32)


def _rms(x, g):
    y = x * lax.rsqrt(jnp.mean(jnp.square(x), axis=-1, keepdims=True) + RMS_EPS)
    return y * g


def _layer_norm(x, g, b):
    mu = jnp.mean(x, axis=-1, keepdims=True)
    xc = x - mu
    var = jnp.mean(jnp.square(xc), axis=-1, keepdims=True)
    return xc * lax.rsqrt(var + LN_EPS) * g + b


def _silu(z):
    return z * (1.0 / (1.0 + jnp.exp(-z)))


def _head_rows(hd, width):
    return slice(hd * width, (hd + 1) * width)


def _first_step():
    return (pl.program_id(0) == 0) & (pl.program_id(1) == 0)


def _const_spec(shape, pipeline_mode=None):
    return pl.BlockSpec(shape, lambda b, i: (0,) * len(shape), pipeline_mode=pipeline_mode)


def _layer_spec(layer, shape, pipeline_mode=None):
    return pl.BlockSpec((None,) + shape, lambda b, i: (layer,) + (0,) * len(shape),
                        pipeline_mode=pipeline_mode)


def _mla_proj_tile(h_ref, cos_t_ref, sin_t_ref, w1_ref, gq_ref, gkv_ref, wuq_ref, wuk_ref, wuv_ref,
                   qt_ref, k_ref, vt_ref, gate_ref, causal_lanes, chunk):
    tm = h_ref.shape[1]
    chunks = [slice(c0, c0 + chunk) for c0 in range(0, tm, chunk)]

    def slot_table(t):
        rows = [jnp.zeros((QK_NOPE, chunk), t.dtype), t, t,
                jnp.zeros((HEAD_SLOT - QK_HEAD, chunk), t.dtype)]
        return jnp.concatenate(rows, axis=0).T

    ps = [_dot(_bf16(h_ref[0, rows, :]), w1_ref[...]) for rows in chunks]

    lora = []
    for rows, p in zip(chunks, ps):
        cqn = _bf16(_rms(p[:, :Q_LORA], gq_ref[...]))
        ckvn = _bf16(_rms(p[:, _OFF_CKV:_OFF_Z], gkv_ref[...]))
        gate_ref[0, rows, :] = _bf16(_silu(p[:, _OFF_Z:_OFF_ROPE]))
        cos_t = cos_t_ref[0, :, rows]
        sin_t = sin_t_ref[0, :, rows]
        lora.append((cqn, ckvn, cos_t, sin_t))

        ab = p[:, _OFF_ROPE:_W1_COLS]
        kr_even = ab * slot_table(cos_t) + pltpu.roll(ab, HEAD_SLOT - QK_ROPE, 1) * slot_table(sin_t)
        if causal_lanes:
            row = lax.broadcasted_iota(jnp.int32, kr_even.shape, 0)
            c = lax.broadcasted_iota(jnp.int32, kr_even.shape, 1) - QK_HEAD
            later_chunk = (c >= 0) & (c < chunk // CHUNK - 1) & (row >> CHUNK_SHIFT > c)
            kr_even = jnp.where(later_chunk, 1.0, kr_even)
        kr_odd = pltpu.roll(kr_even, ODD_SHIFT, 1)
        kn = _dot(ckvn, wuk_ref[...])
        low_half = lax.broadcasted_iota(jnp.int32, kr_even.shape, 1) < QK_NOPE
        for hd in range(0, MLA_HEADS, 2):
            pair = kn[:, hd * QK_NOPE:(hd + 2) * QK_NOPE]
            k_ref[0, hd, rows, :] = _bf16(jnp.where(low_half, pair, kr_even))
            k_ref[0, hd + 1, rows, :] = _bf16(jnp.where(low_half, kr_odd, pair))

    ones_row = (lax.broadcasted_iota(jnp.int32, (BF16_ROWS, chunk), 0) == 0).astype(jnp.bfloat16)
    for ci, (_, ckvn, _, _) in enumerate(lora):
        vt = _dot_nt(wuv_ref[...], ckvn)
        for hd in range(MLA_HEADS):
            vt_ref[0, ci, hd * V_SLOT:hd * V_SLOT + V_HEAD, :] = _bf16(vt[_head_rows(hd, V_HEAD)])
            vt_ref[0, ci, hd * V_SLOT + V_HEAD:(hd + 1) * V_SLOT, :] = ones_row

    scale = QK_HEAD ** -0.5 * LOG2_E
    for rows, (cqn, _, cos_t, sin_t) in zip(chunks, lora):
        qt = _dot_nt(wuq_ref[...], cqn) * scale
        for hd in range(MLA_HEADS):
            base = hd * QK_HEAD
            nope = _bf16(qt[base:base + QK_NOPE])
            x1 = qt[base + QK_NOPE:base + QK_NOPE + HALF_ROPE]
            x2 = qt[base + QK_NOPE + HALF_ROPE:base + QK_HEAD]
            rope = _bf16(jnp.concatenate([x1 * cos_t - x2 * sin_t, x1 * sin_t + x2 * cos_t], axis=0))
            if hd % 2 == 0:
                qt_ref[0, base:base + QK_NOPE, rows] = nope
                qt_ref[0, base + QK_NOPE:base + QK_HEAD, rows] = rope
            else:
                qt_ref[0, base:base + QK_ROPE, rows] = rope
                qt_ref[0, base + QK_ROPE:base + QK_HEAD, rows] = nope


def _mla_proj_kernel(h_ref, cos_ref, sin_ref, hm_ref, cosm_ref, sinm_ref,
                     w1_ref, gq_ref, gkv_ref, wuq_ref, wuk_ref, wuv_ref,
                     qt_ref, k_ref, vt_ref, gate_ref, qtm_ref, km_ref, vtm_ref, gatem_ref, *, layer):
    row = slice(layer, layer + 1)
    weights = (w1_ref, gq_ref.at[row], gkv_ref.at[row], wuq_ref, wuk_ref, wuv_ref)

    @pl.when(_first_step())
    def _():
        _mla_proj_tile(hm_ref, cosm_ref, sinm_ref, *weights, qtm_ref, km_ref, vtm_ref, gatem_ref,
                       causal_lanes=False, chunk=META_PAD)

    _mla_proj_tile(h_ref, cos_ref, sin_ref, *weights, qt_ref, k_ref, vt_ref, gate_ref,
                   causal_lanes=True, chunk=TOKEN_TILE)


def _mla_proj(h, hm, tabs, meta_tabs, w, layer):
    bx, lx, _ = h.shape
    tm = PROJ_TILE
    per_step = tm // TOKEN_TILE
    tile = lambda b, i: (b, i, 0)
    tile_t = lambda b, i: (b, 0, i)

    def outputs(nb, n_tok, n_tiles, t):
        return [
            jax.ShapeDtypeStruct((nb, QK_WIDTH, n_tok), jnp.bfloat16),
            jax.ShapeDtypeStruct((nb, MLA_HEADS, n_tok, HEAD_SLOT), jnp.bfloat16),
            jax.ShapeDtypeStruct((nb, n_tiles, V_WIDTH, t), jnp.bfloat16),
            jax.ShapeDtypeStruct((nb, n_tok, MLA_WIDTH), jnp.bfloat16),
        ]

    res = pl.pallas_call(
        functools.partial(_mla_proj_kernel, layer=layer),
        grid=(bx, lx // tm),
        in_specs=[
            pl.BlockSpec((1, tm, D_MODEL), tile),
            pl.BlockSpec((1, HALF_ROPE, tm), tile_t),
            pl.BlockSpec((1, HALF_ROPE, tm), tile_t),
            _const_spec((1, META_PAD, D_MODEL)),
            _const_spec((1, HALF_ROPE, META_PAD)),
            _const_spec((1, HALF_ROPE, META_PAD)),
            _const_spec((D_MODEL, _W1_COLS)),
            _const_spec(w["gq"].shape),
            _const_spec(w["gkv"].shape),
            _layer_spec(layer, (QK_WIDTH, Q_LORA)),
            _layer_spec(layer, (KV_LORA, MLA_HEADS * QK_NOPE)),
            _layer_spec(layer, (MLA_WIDTH, KV_LORA)),
        ],
        out_specs=[
            pl.BlockSpec((1, QK_WIDTH, tm), tile_t),
            pl.BlockSpec((1, MLA_HEADS, tm, HEAD_SLOT), lambda b, i: (b, 0, i, 0)),
            pl.BlockSpec((1, per_step, V_WIDTH, TOKEN_TILE), lambda b, i: (b, i, 0, 0)),
            pl.BlockSpec((1, tm, MLA_WIDTH), tile),
            _const_spec((1, QK_WIDTH, META_PAD)),
            _const_spec((1, MLA_HEADS, META_PAD, HEAD_SLOT)),
            _const_spec((1, 1, V_WIDTH, META_PAD)),
            _const_spec((1, META_PAD, MLA_WIDTH)),
        ],
        out_shape=outputs(bx, lx, lx // TOKEN_TILE, TOKEN_TILE) + outputs(1, META_PAD, 1, META_PAD),
        compiler_params=pltpu.CompilerParams(
            dimension_semantics=("arbitrary", "arbitrary"), vmem_limit_bytes=VMEM_LIMIT),
        name="mla_proj",
    )(h, *tabs, hm, *meta_tabs, w["w1"], w["gq"], w["gkv"], w["wuq_t"], w["wuk"], w["wuv_t"])
    return res[:4], res[4:]


def _softmax_start(s):
    m = jnp.max(s, axis=0, keepdims=True)
    return m, jnp.exp2(s - m)


def _normalize(acc):
    return acc[:V_HEAD] / acc[V_HEAD:V_HEAD + 1]


def _q_operand(q_ref, hd, mask):
    q = q_ref[0, _head_rows(hd, QK_HEAD), :]
    zero = jnp.zeros((BF16_ROWS, q.shape[1]), q.dtype)
    spare = [zero if mask is None else mask, zero]
    parts = [q] + spare if hd % 2 == 0 else [q[:QK_ROPE]] + spare + [q[QK_ROPE:]]
    return jnp.concatenate(parts, axis=0)


def _attn_kernel(qt_ref, qn_ref, kb_ref, vb_ref, qtm_ref, km_ref, vm_ref, *rest, kinds):
    n = len(kinds)
    _convert(kinds, rest[:n], rest[n + 2:2 * n + 2])
    _attn_step(qt_ref, qn_ref, kb_ref, vb_ref, qtm_ref, km_ref, vm_ref, *rest[n:n + 2], *rest[2 * n + 2:])


def _attn_step(qt_ref, qn_ref, kb_ref, vb_ref, qtm_ref, km_ref, vm_ref, ot_ref, otm_ref,
               kbuf_ref, vbuf_ref, s_ref, s0_ref, sm0_ref, m_ref, acc_ref):
    tk = vb_ref.shape[3]
    tq = qt_ref.shape[2]
    qi = pl.program_id(1)

    kbuf_ref[:, pl.ds(pl.multiple_of(qi * tk, tk), tk), :] = kb_ref[0]
    vbuf_ref[qi] = vb_ref[0, 0]

    row = lax.broadcasted_iota(jnp.int32, (BF16_ROWS, tq), 0)
    qry = lax.broadcasted_iota(jnp.int32, (BF16_ROWS, tq), 1)
    hide = (qry >> CHUNK_SHIFT == row) & (row < tk // CHUNK - 1)

    def mask_rows(is_diagonal):
        return _bf16(jnp.where(hide & is_diagonal, NEG_INF, 0.0))

    def scores(q_ref, hd, j, mask):
        row0 = pl.multiple_of(j * tk, tk)
        return _dot(kbuf_ref[hd, pl.ds(row0, tk), :], _q_operand(q_ref, hd, mask))

    def update(hd, s, j):
        m = m_ref[hd]
        m_new = jnp.maximum(m, jnp.max(s, axis=0, keepdims=True))
        alpha = jnp.exp2(m - m_new)
        p = jnp.exp2(s - m_new)
        m_ref[hd] = m_new
        acc_ref[hd] = alpha * acc_ref[hd] + _dot(vbuf_ref[j, _head_rows(hd, V_SLOT), :], _bf16(p))

    def meta_scores(q_ref):
        return [_dot(km_ref[0, hd], _q_operand(q_ref, hd, None))
                for hd in range(MLA_HEADS)]

    def meta_values(hd, p):
        no_weight = jnp.zeros((META_PAD - N_META, p.shape[1]), jnp.bfloat16)
        return _dot(vm_ref[0, 0, _head_rows(hd, V_SLOT), :],
                    jnp.concatenate([_bf16(p), no_weight], axis=0))

    @pl.when(_first_step())
    def _():
        sm = meta_scores(qtm_ref)
        for hd in range(MLA_HEADS):
            _, p = _softmax_start(sm[hd])
            otm_ref[0, _head_rows(hd, V_HEAD), :] = _bf16(_normalize(meta_values(hd, p)))

    @pl.when(qi == 0)
    def _():
        sm = meta_scores(qt_ref)
        diag_mask = mask_rows(True)
        for hd in range(MLA_HEADS):
            s_ref[0, hd] = scores(qt_ref, hd, 0, diag_mask)
            m, p = _softmax_start(sm[hd])
            m_ref[hd] = m
            acc_ref[hd] = meta_values(hd, p)

    @pl.when(qi > 0)
    def _():
        first_mask = mask_rows(qi == 1)
        for hd in range(MLA_HEADS):
            s_ref[1, hd] = scores(qt_ref, hd, 1, first_mask)
            sm = sm0_ref[hd]
            s = s0_ref[hd]
            m = jnp.maximum(jnp.max(sm, axis=0, keepdims=True), jnp.max(s, axis=0, keepdims=True))
            m_ref[hd] = m
            acc_ref[hd] = meta_values(hd, jnp.exp2(sm - m)) + _dot(
                vbuf_ref[0, _head_rows(hd, V_SLOT), :], _bf16(jnp.exp2(s - m)))

    def pair(j):
        for hd in range(MLA_HEADS):
            s_ref[0, hd] = scores(qt_ref, hd, j + 1, None)
            update(hd, s_ref[1, hd], j)
        last_mask = mask_rows(j + 2 == qi)
        for hd in range(MLA_HEADS):
            s_ref[1, hd] = scores(qt_ref, hd, j + 2, last_mask)
            update(hd, s_ref[0, hd], j + 1)

    def quad_step(jj, carry):
        pair(4 * jj + 1)
        pair(4 * jj + 3)
        return carry

    n_mid = jnp.maximum(qi - 1, 0)
    lax.fori_loop(0, n_mid // 4, quad_step, 0)

    @pl.when(n_mid % 4 >= 2)
    def _():
        pair(4 * (n_mid // 4) + 1)

    @pl.when((qi >= 2) & (qi % 2 == 0))
    def _():
        diag_mask = mask_rows(True)
        for hd in range(MLA_HEADS):
            s_ref[0, hd] = scores(qt_ref, hd, qi, diag_mask)
            update(hd, s_ref[1, hd], qi - 1)

    for hd in range(MLA_HEADS):
        keys = jnp.concatenate([kbuf_ref[hd, 0:tk, :], km_ref[0, hd]], axis=0)
        s_next = _dot(keys, _q_operand(qn_ref, hd, None))
        s0_ref[hd] = s_next[:tk]
        sm0_ref[hd] = s_next[tk:]
        update(hd, s_ref[qi % 2, hd], qi)
    for hd in range(MLA_HEADS):
        ot_ref[0, _head_rows(hd, V_HEAD), :] = _bf16(_normalize(acc_ref[hd]))


def _attention(frames, meta, conversions=()):
    qt, k, vt, _ = frames
    qt_m, k_m, vt_m, _ = meta
    bx, _, lx = qt.shape
    nk, tk = vt.shape[1], vt.shape[3]
    tq = tk
    nq = lx // tq
    step = lambda b, i: b * nq + i
    kinds = tuple(kind for kind, _, _ in conversions)
    extra_in = [p for _, _, p in conversions]
    extra_specs = [pl.BlockSpec((None, CONVERT_ROWS, p.shape[2]), lambda b, i, layer=layer: (layer, step(b, i), 0))
                   for _, layer, p in conversions]
    out_cols = [_W1_COLS if kind == "w1" else p.shape[2] for kind, _, p in conversions]
    extra_out_specs = [pl.BlockSpec((CONVERT_ROWS, c), lambda b, i: (step(b, i), 0)) for c in out_cols]
    extra_out_shape = [jax.ShapeDtypeStruct((D_MODEL, c), jnp.bfloat16) for c in out_cols]
    assert not conversions or bx * nq * CONVERT_ROWS == D_MODEL
    res = pl.pallas_call(
        functools.partial(_attn_kernel, kinds=kinds),
        grid=(bx, nq),
        in_specs=[
            pl.BlockSpec((1, QK_WIDTH, tq), lambda b, i: (b, 0, i)),
            pl.BlockSpec((1, QK_WIDTH, tq), lambda b, i: (b, 0, jnp.minimum(i + 1, nq - 1))),
            pl.BlockSpec((1, MLA_HEADS, tk, HEAD_SLOT), lambda b, i: (b, 0, i, 0)),
            pl.BlockSpec((1, 1, V_WIDTH, tk), lambda b, i: (b, i, 0, 0)),
            _const_spec((1, QK_WIDTH, META_PAD)),
            _const_spec((1, MLA_HEADS, N_META, HEAD_SLOT)),
            _const_spec((1, 1, V_WIDTH, META_PAD)),
        ] + extra_specs,
        out_specs=[
            pl.BlockSpec((1, MLA_WIDTH, tq), lambda b, i: (b, 0, i)),
            _const_spec((1, MLA_WIDTH, META_PAD)),
        ] + extra_out_specs,
        out_shape=[
            jax.ShapeDtypeStruct((bx, MLA_WIDTH, lx), jnp.bfloat16),
            jax.ShapeDtypeStruct((1, MLA_WIDTH, META_PAD), jnp.bfloat16),
        ] + extra_out_shape,
        scratch_shapes=[
            pltpu.VMEM((MLA_HEADS, lx, HEAD_SLOT), jnp.bfloat16),
            pltpu.VMEM((nk, V_WIDTH, tk), jnp.bfloat16),
            pltpu.VMEM((2, MLA_HEADS, tk, tq), jnp.float32),
            pltpu.VMEM((MLA_HEADS, tk, tq), jnp.float32),
            pltpu.VMEM((MLA_HEADS, N_META, tq), jnp.float32),
            pltpu.VMEM((MLA_HEADS, 1, tq), jnp.float32),
            pltpu.VMEM((MLA_HEADS, V_SLOT, tq), jnp.float32),
        ],
        compiler_params=pltpu.CompilerParams(
            dimension_semantics=("arbitrary", "arbitrary"), vmem_limit_bytes=VMEM_LIMIT),
        name="attention",
    )(qt, qt, k, vt, qt_m, k_m, vt_m, *extra_in)
    return res[0], res[1], tuple(res[2:])


def _out_conv_tile(ot_ref, gate_ref, h_ref, out_ref, cu_ref,
                   wo_ref, g1_ref, b1_ref, win_ref, cw_ref, wout_ref, g2_ref, b2_ref, chunk):
    tm = h_ref.shape[1]
    w = CONV_WIDTH
    chunks = [slice(c0, c0 + chunk) for c0 in range(0, tm, chunk)]

    attn = []
    for rows in chunks:
        o = ot_ref[0, :, rows].astype(jnp.float32).T
        y = _bf16(o * gate_ref[0, rows, :].astype(jnp.float32))
        attn.append(_dot(y, wo_ref[...]))

    hs, ys = [], []
    for rows, a in zip(chunks, attn):
        h = _layer_norm(DN_ALPHA * h_ref[0, rows, :] + a, g1_ref[...], b1_ref[...])
        hb = _bf16(h)
        cu = _dot(hb, win_ref[:, w:2 * w]) * _dot(hb, win_ref[:, 2 * w:3 * w])
        c0 = rows.start
        cu_ref[SUBLANES + c0:SUBLANES + c0 + chunk, :] = cu
        conv = cu_ref[SUBLANES - 2 + c0:SUBLANES - 2 + c0 + chunk, :] * cw_ref[0:1, :]
        conv = conv + cu_ref[SUBLANES - 1 + c0:SUBLANES - 1 + c0 + chunk, :] * cw_ref[1:2, :]
        conv = conv + cu * cw_ref[2:3, :]
        hs.append(h)
        ys.append(_dot(hb, win_ref[:, 0:w]) * conv * _silu(_dot(hb, win_ref[:, 3 * w:4 * w])))

    for rows, h, y in zip(chunks, hs, ys):
        r = DN_ALPHA * h + _dot(_bf16(y), wout_ref[...])
        out_ref[0, rows, :] = _layer_norm(r, g2_ref[...], b2_ref[...])


def _out_conv_kernel(ot_ref, gate_ref, h_ref, otm_ref, gatem_ref, hm_ref,
                     wo_ref, lng_ref, lnb_ref, win_ref, cw_ref, wout_ref,
                     out_ref, outm_ref, cu_ref, meta_tail_ref, *, layer):
    tm = h_ref.shape[1]
    mla_row = slice(2 * layer, 2 * layer + 1)
    conv_row = slice(2 * layer + 1, 2 * layer + 2)
    weights = (wo_ref, lng_ref.at[mla_row], lnb_ref.at[mla_row], win_ref, cw_ref, wout_ref,
               lng_ref.at[conv_row], lnb_ref.at[conv_row])

    @pl.when(_first_step())
    def _():
        cu_ref[0:SUBLANES, :] = jnp.zeros((SUBLANES, CONV_WIDTH), jnp.float32)
        _out_conv_tile(otm_ref, gatem_ref, hm_ref, outm_ref, cu_ref, *weights, chunk=META_PAD)
        meta_tail_ref[...] = cu_ref[N_META:N_META + SUBLANES, :]

    @pl.when(pl.program_id(1) == 0)
    def _():
        cu_ref[0:SUBLANES, :] = meta_tail_ref[...]

    _out_conv_tile(ot_ref, gate_ref, h_ref, out_ref, cu_ref, *weights, chunk=TOKEN_TILE)
    cu_ref[0:SUBLANES, :] = cu_ref[tm:tm + SUBLANES, :]


def _out_conv(ot, gate, h, ot_m, gate_m, hm, w, layer):
    bx, lx, _ = h.shape
    tm = CONV_TILE
    tile = lambda b_, i: (b_, i, 0)
    once = pl.Buffered(1)
    return pl.pallas_call(
        functools.partial(_out_conv_kernel, layer=layer),
        grid=(bx, lx // tm),
        in_specs=[
            pl.BlockSpec((1, MLA_WIDTH, tm), lambda b_, i: (b_, 0, i)),
            pl.BlockSpec((1, tm, MLA_WIDTH), tile),
            pl.BlockSpec((1, tm, D_MODEL), tile),
            _const_spec((1, MLA_WIDTH, META_PAD), once),
            _const_spec((1, META_PAD, MLA_WIDTH), once),
            _const_spec((1, META_PAD, D_MODEL), once),
            _const_spec((MLA_WIDTH, D_MODEL), once),
            _const_spec(w["ln_g"].shape, once),
            _const_spec(w["ln_b"].shape, once),
            _const_spec((D_MODEL, 4 * CONV_WIDTH), once),
            _layer_spec(layer, (CONV_K, CONV_WIDTH), once),
            _const_spec((CONV_WIDTH, D_MODEL), once),
        ],
        out_specs=[
            pl.BlockSpec((1, tm, D_MODEL), tile),
            _const_spec((1, META_PAD, D_MODEL)),
        ],
        out_shape=[
            jax.ShapeDtypeStruct(h.shape, jnp.float32),
            jax.ShapeDtypeStruct(hm.shape, jnp.float32),
        ],
        scratch_shapes=[
            pltpu.VMEM((tm + SUBLANES, CONV_WIDTH), jnp.float32),
            pltpu.VMEM((SUBLANES, CONV_WIDTH), jnp.float32),
        ],
        compiler_params=pltpu.CompilerParams(
            dimension_semantics=("arbitrary", "arbitrary"), vmem_limit_bytes=VMEM_LIMIT),
        name="out_conv",
    )(ot, gate, h, ot_m, gate_m, hm,
      w["wo"], w["ln_g"], w["ln_b"], w["win"], w["cw"], w["wout"])


PREP_ROWS = 256
CONVERT_ROWS = 16


def _convert_w1(w_in_ref, w1_out):
    o_kr = Q_LORA + KV_LORA
    o_z = o_kr + QK_ROPE
    x = w_in_ref[...]
    w1_out[:, :o_kr] = _bf16(x[:, :o_kr])
    w1_out[:, _OFF_Z:_OFF_ROPE] = _bf16(x[:, o_z:])
    x1 = x[:, o_kr:o_kr + HALF_ROPE]
    x2 = x[:, o_kr + HALF_ROPE:o_z]
    rope = jnp.concatenate([jnp.zeros((x.shape[0], QK_NOPE), x.dtype), x1, x2, -x2, x1], axis=1)
    w1_out[:, _OFF_ROPE:_W1_COLS] = _bf16(rope)


def _convert(kinds, in_refs, out_refs):
    for kind, src, dst in zip(kinds, in_refs, out_refs):
        if kind == "w1":
            _convert_w1(src, dst)
        else:
            dst[...] = _bf16(src[...])


def _prep_kernel(w_in_ref, wuq_ref, wuk_ref, wuv_ref, w1_out, wuq_t_out, wuk_out, wuv_t_out):
    _convert_w1(w_in_ref, w1_out)

    @pl.when(pl.program_id(0) <= 1)
    def _():
        wuq_t_out[...] = _bf16(wuq_ref[...].T)
        wuk_out[...] = _bf16(wuk_ref[...])
        wuv_t_out[...] = _bf16(wuv_ref[...].T)


def _prep_weights(w_in, w_uq, w_uk, w_uv):
    n = w_in.shape[0]
    assert n == 2
    small = lambda shape: pl.BlockSpec((None,) + shape, lambda r: (jnp.minimum(r, 1), 0, 0))
    bf = lambda *shape: jax.ShapeDtypeStruct(shape, jnp.bfloat16)
    res = pl.pallas_call(
        _prep_kernel,
        grid=(D_MODEL // PREP_ROWS,),
        in_specs=[pl.BlockSpec((None, PREP_ROWS, w_in.shape[2]), lambda r: (0, r, 0)),
                  small((Q_LORA, QK_WIDTH)), small((KV_LORA, MLA_HEADS * QK_NOPE)),
                  small((KV_LORA, MLA_WIDTH))],
        out_specs=[pl.BlockSpec((PREP_ROWS, _W1_COLS), lambda r: (r, 0)),
                   small((QK_WIDTH, Q_LORA)), small((KV_LORA, MLA_HEADS * QK_NOPE)), small((MLA_WIDTH, KV_LORA))],
        out_shape=[bf(D_MODEL, _W1_COLS),
                   bf(n, QK_WIDTH, Q_LORA), bf(n, KV_LORA, MLA_HEADS * QK_NOPE), bf(n, MLA_WIDTH, KV_LORA)],
        compiler_params=pltpu.CompilerParams(
            dimension_semantics=("arbitrary",), vmem_limit_bytes=VMEM_LIMIT),
        name="weight_prep",
    )(w_in, w_uq, w_uk, w_uv)
    return res[0], {"wuq_t": res[1], "wuk": res[2], "wuv_t": res[3]}


def _rope_tables(pos):
    inv_freq = ROPE_BASE ** (-jnp.arange(0, QK_ROPE, 2, dtype=jnp.float32) / QK_ROPE)
    ang = pos[..., None] * inv_freq
    return jnp.swapaxes(jnp.cos(ang), 1, 2), jnp.swapaxes(jnp.sin(ang), 1, 2)


def kernel(x, positions, meta_tokens, ln_g, ln_b, mla_w_in, mla_q_norm_g, mla_w_uq, mla_kv_norm_g,
           mla_w_uk, mla_w_uv, mla_w_o, conv_w_in, conv_w, conv_w_out):
    bsz, seq, _ = x.shape
    assert seq % CONV_TILE == 0 and CONV_TILE % TOKEN_TILE == 0 and TOKEN_TILE % CHUNK == 0
    assert seq % PROJ_TILE == 0 and PROJ_TILE % TOKEN_TILE == 0
    assert DEPTH % 2 == 0
    f32 = jnp.float32

    hf = x.astype(f32)
    hm = jnp.pad(meta_tokens.astype(f32), ((0, META_PAD - N_META), (0, 0)))[None]

    frame_tabs = _rope_tables((positions + N_META).astype(f32))
    meta_pos = jnp.pad(jnp.arange(N_META, dtype=f32), (0, META_PAD - N_META))[None]
    meta_tabs = _rope_tables(meta_pos)

    w1, w = _prep_weights(mla_w_in, mla_w_uq, mla_w_uk, mla_w_uv)
    w.update({"gq": mla_q_norm_g, "gkv": mla_kv_norm_g, "ln_g": ln_g, "ln_b": ln_b, "cw": conv_w, "w1": w1})
    n_pairs = DEPTH // 2
    casts = lambda layer: [("cast", layer, p) for p in (mla_w_o, conv_w_in, conv_w_out)]
    for j in range(n_pairs):
        frames, meta = _mla_proj(hf, hm, frame_tabs, meta_tabs, w, j)
        if j == 0:
            jobs = [c for layer in range(n_pairs) for c in casts(layer)]
            jobs += [("w1", layer, mla_w_in) for layer in range(1, n_pairs)]
        else:
            jobs = []
        ot_f, ot_m, converted = _attention(frames, meta, jobs)
        if j == 0:
            plain = [converted[3 * layer:3 * layer + 3] for layer in range(n_pairs)]
            later_w1 = converted[3 * n_pairs:]
        w.update(dict(zip(("wo", "win", "wout"), plain[j])))
        hf, hm = _out_conv(ot_f, frames[3], hf, ot_m, meta[3], hm, w, j)
        if j + 1 < n_pairs:
            w["w1"] = later_w1[j]
    return hf
```

```python
import functools

import jax
import jax.numpy as jnp
from jax import lax
from jax.experimental import pallas as pl
from jax.experimental.pallas import tpu as pltpu

D_MODEL = 1024
DEPTH = 4
CHUNK = 64
CHUNK_SHIFT = CHUNK.bit_length() - 1
N_META = 16
MLA_HEADS = 16
QK_NOPE = 64
QK_ROPE = 32
HALF_ROPE = QK_ROPE // 2
V_HEAD = 64
Q_LORA = 384
KV_LORA = 256
MLA_WIDTH = MLA_HEADS * V_HEAD
ROPE_BASE = 10000.0
CONV_WIDTH = D_MODEL
CONV_K = 3
DN_ALPHA = (2 * DEPTH) ** 0.25
LN_EPS = 1e-5
RMS_EPS = 1e-6
NEG_INF = -1e30
LOG2_E = 1.4426950408889634

QK_HEAD = QK_NOPE + QK_ROPE
QK_WIDTH = MLA_HEADS * QK_HEAD
HEAD_SLOT = 128
ODD_SHIFT = HEAD_SLOT - QK_NOPE
BF16_ROWS = 16
V_SLOT = V_HEAD + BF16_ROWS
V_WIDTH = MLA_HEADS * V_SLOT
META_PAD = 128
TOKEN_TILE = 256
CONV_TILE = 512
PROJ_TILE = 512
SUBLANES = 8
VMEM_LIMIT = 56 * 1024 * 1024

_OFF_CKV = Q_LORA
_OFF_Z = Q_LORA + KV_LORA
_OFF_ROPE = _OFF_Z + MLA_WIDTH
_W1_COLS = _OFF_ROPE + HEAD_SLOT

_NT = (((1,), (1,)), ((), ()))


def _bf16(x):
    return x.astype(jnp.bfloat16)


def _dot(a, b):
    return jnp.dot(a, b, preferred_element_type=jnp.float32)


def _dot_nt(a, b):
    return lax.dot_general(a, b, _NT, preferred_element_type=jnp.float32)


def _rms(x, g):
    y = x * lax.rsqrt(jnp.mean(jnp.square(x), axis=-1, keepdims=True) + RMS_EPS)
    return y * g


def _layer_norm(x, g, b):
    mu = jnp.mean(x, axis=-1, keepdims=True)
    xc = x - mu
    var = jnp.mean(jnp.square(xc), axis=-1, keepdims=True)
    return xc * lax.rsqrt(var + LN_EPS) * g + b


def _silu(z):
    return z * (1.0 / (1.0 + jnp.exp(-z)))


def _head_rows(hd, width):
    return slice(hd * width, (hd + 1) * width)


def _first_step():
    return (pl.program_id(0) == 0) & (pl.program_id(1) == 0)


def _const_spec(shape, pipeline_mode=None):
    return pl.BlockSpec(shape, lambda b, i: (0,) * len(shape), pipeline_mode=pipeline_mode)


def _layer_spec(layer, shape, pipeline_mode=None):
    return pl.BlockSpec((None,) + shape, lambda b, i: (layer,) + (0,) * len(shape),
                        pipeline_mode=pipeline_mode)


def _mla_proj_tile(h_ref, cos_t_ref, sin_t_ref, w1_ref, gq_ref, gkv_ref, wuq_ref, wuk_ref, wuv_ref,
                   qt_ref, k_ref, vt_ref, gate_ref, causal_lanes, chunk):
    tm = h_ref.shape[1]
    chunks = [slice(c0, c0 + chunk) for c0 in range(0, tm, chunk)]

    def slot_table(t):
        rows = [jnp.zeros((QK_NOPE, chunk), t.dtype), t, t,
                jnp.zeros((HEAD_SLOT - QK_HEAD, chunk), t.dtype)]
        return jnp.concatenate(rows, axis=0).T

    ps = [_dot(_bf16(h_ref[0, rows, :]), w1_ref[...]) for rows in chunks]

    lora = []
    for rows, p in zip(chunks, ps):
        cqn = _bf16(_rms(p[:, :Q_LORA], gq_ref[...]))
        ckvn = _bf16(_rms(p[:, _OFF_CKV:_OFF_Z], gkv_ref[...]))
        gate_ref[0, rows, :] = _bf16(_silu(p[:, _OFF_Z:_OFF_ROPE]))
        cos_t = cos_t_ref[0, :, rows]
        sin_t = sin_t_ref[0, :, rows]
        lora.append((cqn, ckvn, cos_t, sin_t))

        ab = p[:, _OFF_ROPE:_W1_COLS]
        kr_even = ab * slot_table(cos_t) + pltpu.roll(ab, HEAD_SLOT - QK_ROPE, 1) * slot_table(sin_t)
        if causal_lanes:
            row = lax.broadcasted_iota(jnp.int32, kr_even.shape, 0)
            c = lax.broadcasted_iota(jnp.int32, kr_even.shape, 1) - QK_HEAD
            later_chunk = (c >= 0) & (c < chunk // CHUNK - 1) & (row >> CHUNK_SHIFT > c)
            kr_even = jnp.where(later_chunk, 1.0, kr_even)
        kr_odd = pltpu.roll(kr_even, ODD_SHIFT, 1)
        kn = _dot(ckvn, wuk_ref[...])
        low_half = lax.broadcasted_iota(jnp.int32, kr_even.shape, 1) < QK_NOPE
        for hd in range(0, MLA_HEADS, 2):
            pair = kn[:, hd * QK_NOPE:(hd + 2) * QK_NOPE]
            k_ref[0, hd, rows, :] = _bf16(jnp.where(low_half, pair, kr_even))
            k_ref[0, hd + 1, rows, :] = _bf16(jnp.where(low_half, kr_odd, pair))

    ones_row = (lax.broadcasted_iota(jnp.int32, (BF16_ROWS, chunk), 0) == 0).astype(jnp.bfloat16)
    for ci, (_, ckvn, _, _) in enumerate(lora):
        vt = _dot_nt(wuv_ref[...], ckvn)
        for hd in range(MLA_HEADS):
            vt_ref[0, ci, hd * V_SLOT:hd * V_SLOT + V_HEAD, :] = _bf16(vt[_head_rows(hd, V_HEAD)])
            vt_ref[0, ci, hd * V_SLOT + V_HEAD:(hd + 1) * V_SLOT, :] = ones_row

    scale = QK_HEAD ** -0.5 * LOG2_E
    for rows, (cqn, _, cos_t, sin_t) in zip(chunks, lora):
        qt = _dot_nt(wuq_ref[...], cqn) * scale
        for hd in range(MLA_HEADS):
            base = hd * QK_HEAD
            nope = _bf16(qt[base:base + QK_NOPE])
            x1 = qt[base + QK_NOPE:base + QK_NOPE + HALF_ROPE]
            x2 = qt[base + QK_NOPE + HALF_ROPE:base + QK_HEAD]
            rope = _bf16(jnp.concatenate([x1 * cos_t - x2 * sin_t, x1 * sin_t + x2 * cos_t], axis=0))
            if hd % 2 == 0:
                qt_ref[0, base:base + QK_NOPE, rows] = nope
                qt_ref[0, base + QK_NOPE:base + QK_HEAD, rows] = rope
            else:
                qt_ref[0, base:base + QK_ROPE, rows] = rope
                qt_ref[0, base + QK_ROPE:base + QK_HEAD, rows] = nope


def _mla_proj_kernel(h_ref, cos_ref, sin_ref, hm_ref, cosm_ref, sinm_ref,
                     w1_ref, gq_ref, gkv_ref, wuq_ref, wuk_ref, wuv_ref,
                     qt_ref, k_ref, vt_ref, gate_ref, qtm_ref, km_ref, vtm_ref, gatem_ref, *, layer):
    row = slice(layer, layer + 1)
    weights = (w1_ref, gq_ref.at[row], gkv_ref.at[row], wuq_ref, wuk_ref, wuv_ref)

    @pl.when(_first_step())
    def _():
        _mla_proj_tile(hm_ref, cosm_ref, sinm_ref, *weights, qtm_ref, km_ref, vtm_ref, gatem_ref,
                       causal_lanes=False, chunk=META_PAD)

    _mla_proj_tile(h_ref, cos_ref, sin_ref, *weights, qt_ref, k_ref, vt_ref, gate_ref,
                   causal_lanes=True, chunk=TOKEN_TILE)


def _mla_proj(h, hm, tabs, meta_tabs, w, layer):
    bx, lx, _ = h.shape
    tm = PROJ_TILE
    per_step = tm // TOKEN_TILE
    tile = lambda b, i: (b, i, 0)
    tile_t = lambda b, i: (b, 0, i)

    def outputs(nb, n_tok, n_tiles, t):
        return [
            jax.ShapeDtypeStruct((nb, QK_WIDTH, n_tok), jnp.bfloat16),
            jax.ShapeDtypeStruct((nb, MLA_HEADS, n_tok, HEAD_SLOT), jnp.bfloat16),
            jax.ShapeDtypeStruct((nb, n_tiles, V_WIDTH, t), jnp.bfloat16),
            jax.ShapeDtypeStruct((nb, n_tok, MLA_WIDTH), jnp.bfloat16),
        ]

    res = pl.pallas_call(
        functools.partial(_mla_proj_kernel, layer=layer),
        grid=(bx, lx // tm),
        in_specs=[
            pl.BlockSpec((1, tm, D_MODEL), tile),
            pl.BlockSpec((1, HALF_ROPE, tm), tile_t),
            pl.BlockSpec((1, HALF_ROPE, tm), tile_t),
            _const_spec((1, META_PAD, D_MODEL)),
            _const_spec((1, HALF_ROPE, META_PAD)),
            _const_spec((1, HALF_ROPE, META_PAD)),
            _const_spec((D_MODEL, _W1_COLS)),
            _const_spec(w["gq"].shape),
            _const_spec(w["gkv"].shape),
            _layer_spec(layer, (QK_WIDTH, Q_LORA)),
            _layer_spec(layer, (KV_LORA, MLA_HEADS * QK_NOPE)),
            _layer_spec(layer, (MLA_WIDTH, KV_LORA)),
        ],
        out_specs=[
            pl.BlockSpec((1, QK_WIDTH, tm), tile_t),
            pl.BlockSpec((1, MLA_HEADS, tm, HEAD_SLOT), lambda b, i: (b, 0, i, 0)),
            pl.BlockSpec((1, per_step, V_WIDTH, TOKEN_TILE), lambda b, i: (b, i, 0, 0)),
            pl.BlockSpec((1, tm, MLA_WIDTH), tile),
            _const_spec((1, QK_WIDTH, META_PAD)),
            _const_spec((1, MLA_HEADS, META_PAD, HEAD_SLOT)),
            _const_spec((1, 1, V_WIDTH, META_PAD)),
            _const_spec((1, META_PAD, MLA_WIDTH)),
        ],
        out_shape=outputs(bx, lx, lx // TOKEN_TILE, TOKEN_TILE) + outputs(1, META_PAD, 1, META_PAD),
        compiler_params=pltpu.CompilerParams(
            dimension_semantics=("arbitrary", "arbitrary"), vmem_limit_bytes=VMEM_LIMIT),
        name="mla_proj",
    )(h, *tabs, hm, *meta_tabs, w["w1"], w["gq"], w["gkv"], w["wuq_t"], w["wuk"], w["wuv_t"])
    return res[:4], res[4:]


def _softmax_start(s):
    m = jnp.max(s, axis=0, keepdims=True)
    return m, jnp.exp2(s - m)


def _normalize(acc):
    return acc[:V_HEAD] / acc[V_HEAD:V_HEAD + 1]


def _q_operand(q_ref, hd, mask):
    q = q_ref[0, _head_rows(hd, QK_HEAD), :]
    zero = jnp.zeros((BF16_ROWS, q.shape[1]), q.dtype)
    spare = [zero if mask is None else mask, zero]
    parts = [q] + spare if hd % 2 == 0 else [q[:QK_ROPE]] + spare + [q[QK_ROPE:]]
    return jnp.concatenate(parts, axis=0)


def _attn_kernel(qt_ref, qn_ref, kb_ref, vb_ref, qtm_ref, km_ref, vm_ref, *rest, kinds):
    n = len(kinds)
    _convert(kinds, rest[:n], rest[n + 2:2 * n + 2])
    _attn_step(qt_ref, qn_ref, kb_ref, vb_ref, qtm_ref, km_ref, vm_ref, *rest[n:n + 2], *rest[2 * n + 2:])


def _attn_step(qt_ref, qn_ref, kb_ref, vb_ref, qtm_ref, km_ref, vm_ref, ot_ref, otm_ref,
               kbuf_ref, vbuf_ref, s_ref, s0_ref, sm0_ref, m_ref, acc_ref):
    tk = vb_ref.shape[3]
    tq = qt_ref.shape[2]
    qi = pl.program_id(1)

    kbuf_ref[:, pl.ds(pl.multiple_of(qi * tk, tk), tk), :] = kb_ref[0]
    vbuf_ref[qi] = vb_ref[0, 0]

    row = lax.broadcasted_iota(jnp.int32, (BF16_ROWS, tq), 0)
    qry = lax.broadcasted_iota(jnp.int32, (BF16_ROWS, tq), 1)
    hide = (qry >> CHUNK_SHIFT == row) & (row < tk // CHUNK - 1)

    def mask_rows(is_diagonal):
        return _bf16(jnp.where(hide & is_diagonal, NEG_INF, 0.0))

    def scores(q_ref, hd, j, mask):
        row0 = pl.multiple_of(j * tk, tk)
        return _dot(kbuf_ref[hd, pl.ds(row0, tk), :], _q_operand(q_ref, hd, mask))

    def update(hd, s, j):
        m = m_ref[hd]
        m_new = jnp.maximum(m, jnp.max(s, axis=0, keepdims=True))
        alpha = jnp.exp2(m - m_new)
        p = jnp.exp2(s - m_new)
        m_ref[hd] = m_new
        acc_ref[hd] = alpha * acc_ref[hd] + _dot(vbuf_ref[j, _head_rows(hd, V_SLOT), :], _bf16(p))

    def meta_scores(q_ref):
        return [_dot(km_ref[0, hd], _q_operand(q_ref, hd, None))
                for hd in range(MLA_HEADS)]

    def meta_values(hd, p):
        no_weight = jnp.zeros((META_PAD - N_META, p.shape[1]), jnp.bfloat16)
        return _dot(vm_ref[0, 0, _head_rows(hd, V_SLOT), :],
                    jnp.concatenate([_bf16(p), no_weight], axis=0))

    @pl.when(_first_step())
    def _():
        sm = meta_scores(qtm_ref)
        for hd in range(MLA_HEADS):
            _, p = _softmax_start(sm[hd])
            otm_ref[0, _head_rows(hd, V_HEAD), :] = _bf16(_normalize(meta_values(hd, p)))

    @pl.when(qi == 0)
    def _():
        sm = meta_scores(qt_ref)
        diag_mask = mask_rows(True)
        for hd in range(MLA_HEADS):
            s_ref[0, hd] = scores(qt_ref, hd, 0, diag_mask)
            m, p = _softmax_start(sm[hd])
            m_ref[hd] = m
            acc_ref[hd] = meta_values(hd, p)

    @pl.when(qi > 0)
    def _():
        first_mask = mask_rows(qi == 1)
        for hd in range(MLA_HEADS):
            s_ref[1, hd] = scores(qt_ref, hd, 1, first_mask)
            sm = sm0_ref[hd]
            s = s0_ref[hd]
            m = jnp.maximum(jnp.max(sm, axis=0, keepdims=True), jnp.max(s, axis=0, keepdims=True))
            m_ref[hd] = m
            acc_ref[hd] = meta_values(hd, jnp.exp2(sm - m)) + _dot(
                vbuf_ref[0, _head_rows(hd, V_SLOT), :], _bf16(jnp.exp2(s - m)))

    def pair(j):
        for hd in range(MLA_HEADS):
            s_ref[0, hd] = scores(qt_ref, hd, j + 1, None)
            update(hd, s_ref[1, hd], j)
        last_mask = mask_rows(j + 2 == qi)
        for hd in range(MLA_HEADS):
            s_ref[1, hd] = scores(qt_ref, hd, j + 2, last_mask)
            update(hd, s_ref[0, hd], j + 1)

    def quad_step(jj, carry):
        pair(4 * jj + 1)
        pair(4 * jj + 3)
        return carry

    n_mid = jnp.maximum(qi - 1, 0)
    lax.fori_loop(0, n_mid // 4, quad_step, 0)

    @pl.when(n_mid % 4 >= 2)
    def _():
        pair(4 * (n_mid // 4) + 1)

    @pl.when((qi >= 2) & (qi % 2 == 0))
    def _():
        diag_mask = mask_rows(True)
        for hd in range(MLA_HEADS):
            s_ref[0, hd] = scores(qt_ref, hd, qi, diag_mask)
            update(hd, s_ref[1, hd], qi - 1)

    for hd in range(MLA_HEADS):
        keys = jnp.concatenate([kbuf_ref[hd, 0:tk, :], km_ref[0, hd]], axis=0)
        s_next = _dot(keys, _q_operand(qn_ref, hd, None))
        s0_ref[hd] = s_next[:tk]
        sm0_ref[hd] = s_next[tk:]
        update(hd, s_ref[qi % 2, hd], qi)
    for hd in range(MLA_HEADS):
        ot_ref[0, _head_rows(hd, V_HEAD), :] = _bf16(_normalize(acc_ref[hd]))


def _attention(frames, meta, conversions=()):
    qt, k, vt, _ = frames
    qt_m, k_m, vt_m, _ = meta
    bx, _, lx = qt.shape
    nk, tk = vt.shape[1], vt.shape[3]
    tq = tk
    nq = lx // tq
    step = lambda b, i: b * nq + i
    kinds = tuple(kind for kind, _, _ in conversions)
    extra_in = [p for _, _, p in conversions]
    extra_specs = [pl.BlockSpec((None, CONVERT_ROWS, p.shape[2]), lambda b, i, layer=layer: (layer, step(b, i), 0))
                   for _, layer, p in conversions]
    out_cols = [_W1_COLS if kind == "w1" else p.shape[2] for kind, _, p in conversions]
    extra_out_specs = [pl.BlockSpec((CONVERT_ROWS, c), lambda b, i: (step(b, i), 0)) for c in out_cols]
    extra_out_shape = [jax.ShapeDtypeStruct((D_MODEL, c), jnp.bfloat16) for c in out_cols]
    assert not conversions or bx * nq * CONVERT_ROWS == D_MODEL
    res = pl.pallas_call(
        functools.partial(_attn_kernel, kinds=kinds),
        grid=(bx, nq),
        in_specs=[
            pl.BlockSpec((1, QK_WIDTH, tq), lambda b, i: (b, 0, i)),
            pl.BlockSpec((1, QK_WIDTH, tq), lambda b, i: (b, 0, jnp.minimum(i + 1, nq - 1))),
            pl.BlockSpec((1, MLA_HEADS, tk, HEAD_SLOT), lambda b, i: (b, 0, i, 0)),
            pl.BlockSpec((1, 1, V_WIDTH, tk), lambda b, i: (b, i, 0, 0)),
            _const_spec((1, QK_WIDTH, META_PAD)),
            _const_spec((1, MLA_HEADS, N_META, HEAD_SLOT)),
            _const_spec((1, 1, V_WIDTH, META_PAD)),
        ] + extra_specs,
        out_specs=[
            pl.BlockSpec((1, MLA_WIDTH, tq), lambda b, i: (b, 0, i)),
            _const_spec((1, MLA_WIDTH, META_PAD)),
        ] + extra_out_specs,
        out_shape=[
            jax.ShapeDtypeStruct((bx, MLA_WIDTH, lx), jnp.bfloat16),
            jax.ShapeDtypeStruct((1, MLA_WIDTH, META_PAD), jnp.bfloat16),
        ] + extra_out_shape,
        scratch_shapes=[
            pltpu.VMEM((MLA_HEADS, lx, HEAD_SLOT), jnp.bfloat16),
            pltpu.VMEM((nk, V_WIDTH, tk), jnp.bfloat16),
            pltpu.VMEM((2, MLA_HEADS, tk, tq), jnp.float32),
            pltpu.VMEM((MLA_HEADS, tk, tq), jnp.float32),
            pltpu.VMEM((MLA_HEADS, N_META, tq), jnp.float32),
            pltpu.VMEM((MLA_HEADS, 1, tq), jnp.float32),
            pltpu.VMEM((MLA_HEADS, V_SLOT, tq), jnp.float32),
        ],
        compiler_params=pltpu.CompilerParams(
            dimension_semantics=("arbitrary", "arbitrary"), vmem_limit_bytes=VMEM_LIMIT),
        name="attention",
    )(qt, qt, k, vt, qt_m, k_m, vt_m, *extra_in)
    return res[0], res[1], tuple(res[2:])


def _out_conv_tile(ot_ref, gate_ref, h_ref, out_ref, cu_ref,
                   wo_ref, g1_ref, b1_ref, win_ref, cw_ref, wout_ref, g2_ref, b2_ref, chunk):
    tm = h_ref.shape[1]
    w = CONV_WIDTH
    chunks = [slice(c0, c0 + chunk) for c0 in range(0, tm, chunk)]

    attn = []
    for rows in chunks:
        o = ot_ref[0, :, rows].astype(jnp.float32).T
        y = _bf16(o * gate_ref[0, rows, :].astype(jnp.float32))
        attn.append(_dot(y, wo_ref[...]))

    hs, ys = [], []
    for rows, a in zip(chunks, attn):
        h = _layer_norm(DN_ALPHA * h_ref[0, rows, :] + a, g1_ref[...], b1_ref[...])
        hb = _bf16(h)
        cu = _dot(hb, win_ref[:, w:2 * w]) * _dot(hb, win_ref[:, 2 * w:3 * w])
        c0 = rows.start
        cu_ref[SUBLANES + c0:SUBLANES + c0 + chunk, :] = cu
        conv = cu_ref[SUBLANES - 2 + c0:SUBLANES - 2 + c0 + chunk, :] * cw_ref[0:1, :]
        conv = conv + cu_ref[SUBLANES - 1 + c0:SUBLANES - 1 + c0 + chunk, :] * cw_ref[1:2, :]
        conv = conv + cu * cw_ref[2:3, :]
        hs.append(h)
        ys.append(_dot(hb, win_ref[:, 0:w]) * conv * _silu(_dot(hb, win_ref[:, 3 * w:4 * w])))

    for rows, h, y in zip(chunks, hs, ys):
        r = DN_ALPHA * h + _dot(_bf16(y), wout_ref[...])
        out_ref[0, rows, :] = _layer_norm(r, g2_ref[...], b2_ref[...])


def _out_conv_kernel(ot_ref, gate_ref, h_ref, otm_ref, gatem_ref, hm_ref,
                     wo_ref, lng_ref, lnb_ref, win_ref, cw_ref, wout_ref,
                     out_ref, outm_ref, cu_ref, meta_tail_ref, *, layer):
    tm = h_ref.shape[1]
    mla_row = slice(2 * layer, 2 * layer + 1)
    conv_row = slice(2 * layer + 1, 2 * layer + 2)
    weights = (wo_ref, lng_ref.at[mla_row], lnb_ref.at[mla_row], win_ref, cw_ref, wout_ref,
               lng_ref.at[conv_row], lnb_ref.at[conv_row])

    @pl.when(_first_step())
    def _():
        cu_ref[0:SUBLANES, :] = jnp.zeros((SUBLANES, CONV_WIDTH), jnp.float32)
        _out_conv_tile(otm_ref, gatem_ref, hm_ref, outm_ref, cu_ref, *weights, chunk=META_PAD)
        meta_tail_ref[...] = cu_ref[N_META:N_META + SUBLANES, :]

    @pl.when(pl.program_id(1) == 0)
    def _():
        cu_ref[0:SUBLANES, :] = meta_tail_ref[...]

    _out_conv_tile(ot_ref, gate_ref, h_ref, out_ref, cu_ref, *weights, chunk=TOKEN_TILE)
    cu_ref[0:SUBLANES, :] = cu_ref[tm:tm + SUBLANES, :]


def _out_conv(ot, gate, h, ot_m, gate_m, hm, w, layer):
    bx, lx, _ = h.shape
    tm = CONV_TILE
    tile = lambda b_, i: (b_, i, 0)
    once = pl.Buffered(1)
    return pl.pallas_call(
        functools.partial(_out_conv_kernel, layer=layer),
        grid=(bx, lx // tm),
        in_specs=[
            pl.BlockSpec((1, MLA_WIDTH, tm), lambda b_, i: (b_, 0, i)),
            pl.BlockSpec((1, tm, MLA_WIDTH), tile),
            pl.BlockSpec((1, tm, D_MODEL), tile),
            _const_spec((1, MLA_WIDTH, META_PAD), once),
            _const_spec((1, META_PAD, MLA_WIDTH), once),
            _const_spec((1, META_PAD, D_MODEL), once),
            _const_spec((MLA_WIDTH, D_MODEL), once),
            _const_spec(w["ln_g"].shape, once),
            _const_spec(w["ln_b"].shape, once),
            _const_spec((D_MODEL, 4 * CONV_WIDTH), once),
            _layer_spec(layer, (CONV_K, CONV_WIDTH), once),
            _const_spec((CONV_WIDTH, D_MODEL), once),
        ],
        out_specs=[
            pl.BlockSpec((1, tm, D_MODEL), tile),
            _const_spec((1, META_PAD, D_MODEL)),
        ],
        out_shape=[
            jax.ShapeDtypeStruct(h.shape, jnp.float32),
            jax.ShapeDtypeStruct(hm.shape, jnp.float32),
        ],
        scratch_shapes=[
            pltpu.VMEM((tm + SUBLANES, CONV_WIDTH), jnp.float32),
            pltpu.VMEM((SUBLANES, CONV_WIDTH), jnp.float32),
        ],
        compiler_params=pltpu.CompilerParams(
            dimension_semantics=("arbitrary", "arbitrary"), vmem_limit_bytes=VMEM_LIMIT),
        name="out_conv",
    )(ot, gate, h, ot_m, gate_m, hm,
      w["wo"], w["ln_g"], w["ln_b"], w["win"], w["cw"], w["wout"])


PREP_ROWS = 256
CONVERT_ROWS = 16


def _convert_w1(w_in_ref, w1_out):
    o_kr = Q_LORA + KV_LORA
    o_z = o_kr + QK_ROPE
    x = w_in_ref[...]
    w1_out[:, :o_kr] = _bf16(x[:, :o_kr])
    w1_out[:, _OFF_Z:_OFF_ROPE] = _bf16(x[:, o_z:])
    x1 = x[:, o_kr:o_kr + HALF_ROPE]
    x2 = x[:, o_kr + HALF_ROPE:o_z]
    rope = jnp.concatenate([jnp.zeros((x.shape[0], QK_NOPE), x.dtype), x1, x2, -x2, x1], axis=1)
    w1_out[:, _OFF_ROPE:_W1_COLS] = _bf16(rope)


def _convert(kinds, in_refs, out_refs):
    for kind, src, dst in zip(kinds, in_refs, out_refs):
        if kind == "w1":
            _convert_w1(src, dst)
        else:
            dst[...] = _bf16(src[...])


def _prep_kernel(w_in_ref, wuq_ref, wuk_ref, wuv_ref, w1_out, wuq_t_out, wuk_out, wuv_t_out):
    _convert_w1(w_in_ref, w1_out)

    @pl.when(pl.program_id(0) <= 1)
    def _():
        wuq_t_out[...] = _bf16(wuq_ref[...].T)
        wuk_out[...] = _bf16(wuk_ref[...])
        wuv_t_out[...] = _bf16(wuv_ref[...].T)


def _prep_weights(w_in, w_uq, w_uk, w_uv):
    n = w_in.shape[0]
    assert n == 2
    small = lambda shape: pl.BlockSpec((None,) + shape, lambda r: (jnp.minimum(r, 1), 0, 0))
    bf = lambda *shape: jax.ShapeDtypeStruct(shape, jnp.bfloat16)
    res = pl.pallas_call(
        _prep_kernel,
        grid=(D_MODEL // PREP_ROWS,),
        in_specs=[pl.BlockSpec((None, PREP_ROWS, w_in.shape[2]), lambda r: (0, r, 0)),
                  small((Q_LORA, QK_WIDTH)), small((KV_LORA, MLA_HEADS * QK_NOPE)),
                  small((KV_LORA, MLA_WIDTH))],
        out_specs=[pl.BlockSpec((PREP_ROWS, _W1_COLS), lambda r: (r, 0)),
                   small((QK_WIDTH, Q_LORA)), small((KV_LORA, MLA_HEADS * QK_NOPE)), small((MLA_WIDTH, KV_LORA))],
        out_shape=[bf(D_MODEL, _W1_COLS),
                   bf(n, QK_WIDTH, Q_LORA), bf(n, KV_LORA, MLA_HEADS * QK_NOPE), bf(n, MLA_WIDTH, KV_LORA)],
        compiler_params=pltpu.CompilerParams(
            dimension_semantics=("arbitrary",), vmem_limit_bytes=VMEM_LIMIT),
        name="weight_prep",
    )(w_in, w_uq, w_uk, w_uv)
    return res[0], {"wuq_t": res[1], "wuk": res[2], "wuv_t": res[3]}


def _rope_tables(pos):
    inv_freq = ROPE_BASE ** (-jnp.arange(0, QK_ROPE, 2, dtype=jnp.float32) / QK_ROPE)
    ang = pos[..., None] * inv_freq
    return jnp.swapaxes(jnp.cos(ang), 1, 2), jnp.swapaxes(jnp.sin(ang), 1, 2)


def kernel(x, positions, meta_tokens, ln_g, ln_b, mla_w_in, mla_q_norm_g, mla_w_uq, mla_kv_norm_g,
           mla_w_uk, mla_w_uv, mla_w_o, conv_w_in, conv_w, conv_w_out):
    bsz, seq, _ = x.shape
    assert seq % CONV_TILE == 0 and CONV_TILE % TOKEN_TILE == 0 and TOKEN_TILE % CHUNK == 0
    assert seq % PROJ_TILE == 0 and PROJ_TILE % TOKEN_TILE == 0
    assert DEPTH % 2 == 0
    f32 = jnp.float32

    hf = x.astype(f32)
    hm = jnp.pad(meta_tokens.astype(f32), ((0, META_PAD - N_META), (0, 0)))[None]

    frame_tabs = _rope_tables((positions + N_META).astype(f32))
    meta_pos = jnp.pad(jnp.arange(N_META, dtype=f32), (0, META_PAD - N_META))[None]
    meta_tabs = _rope_tables(meta_pos)

    w1, w = _prep_weights(mla_w_in, mla_w_uq, mla_w_uk, mla_w_uv)
    w.update({"gq": mla_q_norm_g, "gkv": mla_kv_norm_g, "ln_g": ln_g, "ln_b": ln_b, "cw": conv_w, "w1": w1})
    n_pairs = DEPTH // 2
    casts = lambda layer: [("cast", layer, p) for p in (mla_w_o, conv_w_in, conv_w_out)]
    for j in range(n_pairs):
        frames, meta = _mla_proj(hf, hm, frame_tabs, meta_tabs, w, j)
        jobs = casts(j) + ([("w1", j + 1, mla_w_in)] if j + 1 < n_pairs else [])
        ot_f, ot_m, converted = _attention(frames, meta, jobs)
        w.update(dict(zip(("wo", "win", "wout"), converted[:3])))
        hf, hm = _out_conv(ot_f, frames[3], hf, ot_m, meta[3], hm, w, j)
        if j + 1 < n_pairs:
            w["w1"] = converted[3]
    return hf
```
